```python
import math, functools
import jax, jax.numpy as jnp
from jax import lax
import numpy as np

D_MODEL = 1024
BATCH = 16
SEQ = 2048
DEPTH = 2
DEC_BATCH = 8
DEC_SEQ = 8192
PAST_LEN = 128

HEAD_DIM = 64
ATTN_WIDTH = D_MODEL // 2
N_ATTN_HEADS = ATTN_WIDTH // HEAD_DIM
SSM_WIDTH = D_MODEL - ATTN_WIDTH
SSM_GROUP = 16
N_SSM_GROUPS = SSM_WIDTH // SSM_GROUP
SSM_STATE = 64
IN_WIDTH = 3 * ATTN_WIDTH + SSM_WIDTH
DILATED_BRANCHES = ((128, 1), (512, 4), (2048, 16))
BAND_BLOCK = 64
N_REL_BUCKETS = 32
REL_MAX_DISTANCE = 1024
D_FF = 2816
N_EXPERTS = 8
TOP_K = 2
TOKEN_CHUNK = 1024
NORM_EPS = 1e-6
MASK_VALUE = -1e30

kernel_name = 'hymba_dilated_attn_s5_moe_encoder'


def _rmsnorm(x, g):
    xf = x.astype(jnp.float32)
    y = xf * lax.rsqrt(jnp.mean(xf * xf, axis=-1, keepdims=True) + NORM_EPS)
    return (y * g.astype(jnp.float32)).astype(x.dtype)


def _t5_bucket(rel):
    nb = N_REL_BUCKETS // 2
    max_exact = nb // 2
    ret = (rel > 0).astype(np.int32) * nb
    n = np.abs(rel)
    large = max_exact + (np.log(np.maximum(n, 1) / max_exact)
                         / np.log(REL_MAX_DISTANCE / max_exact) * (nb - max_exact)).astype(np.int32)
    large = np.minimum(large, nb - 1)
    return (ret + np.where(n < max_exact, n, large)).astype(np.int32)


def _dilated_branch(q, k, v, rel_table, window, dilation):
    S, H, E = q.shape
    half = window // (2 * dilation)
    L = S // dilation
    nb = -(-L // BAND_BLOCK)
    Lp = nb * BAND_BLOCK
    qb = jnp.pad(q.reshape(L, dilation, H, E), ((0, Lp - L), (0, 0), (0, 0), (0, 0)))
    qb = qb.reshape(nb, BAND_BLOCK, dilation, H, E)

    def band(t):
        tp = jnp.pad(t.reshape(L, dilation, H, E),
                     ((BAND_BLOCK, Lp - L + BAND_BLOCK), (0, 0), (0, 0), (0, 0)))
        tp = tp.reshape(nb + 2, BAND_BLOCK, dilation, H, E)
        return jnp.concatenate([tp[:-2], tp[1:-1], tp[2:]], axis=1)

    kw, vw = band(k), band(v)
    i = np.arange(BAND_BLOCK)[:, None]
    j = np.arange(3 * BAND_BLOCK)[None, :]
    rel = j - BAND_BLOCK - i
    bias = rel_table[jnp.asarray(_t5_bucket(rel * dilation))]
    bias = jnp.transpose(bias, (2, 0, 1)).astype(jnp.float32)
    kpos = np.arange(nb)[:, None, None] * BAND_BLOCK + j[None] - BAND_BLOCK
    valid = (np.abs(rel)[None] <= half) & (kpos >= 0) & (kpos < L)

    logits = jnp.einsum('nqrhe,nkrhe->nrhqk', qb, kw,
                        preferred_element_type=jnp.float32) * (E ** -0.5) + bias
    logits = jnp.where(jnp.asarray(valid)[:, None, None], logits, MASK_VALUE)
    m = jnp.max(logits, axis=-1, keepdims=True)
    p = jnp.exp(logits - m)
    s = jnp.sum(p, axis=-1, keepdims=True)
    o = jnp.einsum('nrhqk,nkrhe->nqrhe', p, vw.astype(jnp.float32))
    o = o / jnp.transpose(s, (0, 3, 1, 2, 4))
    o = o.reshape(Lp, dilation, H, E)[:L].reshape(S, H, E)

    def rows(t):
        t = jnp.transpose(t[..., 0], (0, 3, 1, 2))
        return t.reshape(Lp, dilation, H)[:L].reshape(S, H)

    return o, rows(m), rows(s)


def _dilated_attention(q, k, v, rel_table):
    outs = [_dilated_branch(q, k, v, rel_table, w, d) for (w, d) in DILATED_BRANCHES]
    m_all = functools.reduce(jnp.maximum, [m for (_, m, _) in outs])
    weights = [s * jnp.exp(m - m_all) for (_, m, s) in outs]
    num = sum(wt[..., None] * o for wt, (o, _, _) in zip(weights, outs))
    return num / sum(weights)[..., None]


def _ssm_combine(e1, e2):
    a1, b1 = e1
    a2, b2 = e2
    return a1 * a2, a2 * b1 + b2


def _ssm_scan(uc, lam_re, lam_im, log_step, b_re, b_im, c_re, c_im, reverse):
    f32 = jnp.float32
    lam = lax.complex(lam_re.astype(f32), lam_im.astype(f32))
    step = jnp.exp(log_step.astype(f32))[:, None]
    a_bar = jnp.exp(lam * step)
    b_bar = ((a_bar - 1.0) / lam)[:, :, None] * lax.complex(b_re.astype(f32), b_im.astype(f32))
    bu = jnp.einsum('sgc,gnc->sgn', uc, b_bar)
    a = jnp.broadcast_to(a_bar, bu.shape)
    _, xs = lax.associative_scan(_ssm_combine, (a, bu), reverse=reverse, axis=0)
    c = lax.complex(c_re.astype(f32), c_im.astype(f32))
    return jnp.einsum('gcn,sgn->sgc', c, xs).real


def _s5_mixer(u, lp):
    f32 = jnp.float32
    S = u.shape[0]
    uf = u.astype(f32)
    uc = uf.reshape(S, N_SSM_GROUPS, SSM_GROUP).astype(jnp.complex64)
    y = uf * lp['ssm_d'].astype(f32)
    for direction, reverse in ((0, False), (1, True)):
        y = y + _ssm_scan(uc, lp['ssm_lambda_re'][direction], lp['ssm_lambda_im'][direction],
                          lp['ssm_log_step'][direction], lp['ssm_b_re'][direction],
                          lp['ssm_b_im'][direction], lp['ssm_c_re'][direction],
                          lp['ssm_c_im'][direction], reverse).reshape(S, SSM_WIDTH)
    y = jax.nn.gelu(y)
    gate = jax.nn.sigmoid(y @ lp['glu_w'].astype(f32) + lp['glu_b'].astype(f32))
    return (y * gate).astype(u.dtype)


def _token_mixer(x_seq, lp):
    S = x_seq.shape[0]
    h = _rmsnorm(x_seq, lp['norm_mix'])
    z = h @ lp['w_in']
    q = z[:, :ATTN_WIDTH].reshape(S, N_ATTN_HEADS, HEAD_DIM)
    k = z[:, ATTN_WIDTH:2 * ATTN_WIDTH].reshape(S, N_ATTN_HEADS, HEAD_DIM)
    v = z[:, 2 * ATTN_WIDTH:3 * ATTN_WIDTH].reshape(S, N_ATTN_HEADS, HEAD_DIM)
    u = z[:, 3 * ATTN_WIDTH:]
    attn = _dilated_attention(q, k, v, lp['rel_bias']).reshape(S, ATTN_WIDTH).astype(x_seq.dtype)
    ssm = _s5_mixer(u, lp)
    mixed = jnp.concatenate([_rmsnorm(attn, lp['attn_out_norm']),
                             _rmsnorm(ssm, lp['ssm_out_norm'])], axis=-1)
    return mixed @ lp['w_out']


def _dense_swiglu(xc, g, w1, w3, w2):
    h = _rmsnorm(xc, g)
    return (jax.nn.silu(h @ w1) * (h @ w3)) @ w2


def _moe_swiglu(xc, g, router, w1, w3, w2):
    h = _rmsnorm(xc, g)
    logits = jnp.dot(h, router, preferred_element_type=jnp.float32)
    top_val, top_idx = lax.top_k(logits, TOP_K)
    top_w = jax.nn.softmax(top_val, axis=-1)
    gate = jnp.sum(jax.nn.one_hot(top_idx, N_EXPERTS, dtype=jnp.float32) * top_w[..., None], axis=1)
    a = jnp.einsum('cd,edf->cef', h, w1)
    b = jnp.einsum('cd,edf->cef', h, w3)
    y = jnp.einsum('cef,efd->ced', jax.nn.silu(a) * b, w2)
    return jnp.einsum('ced,ce->cd', y, gate.astype(y.dtype))


def _token_chunks(fn, x):
    B, S, D = x.shape
    n = B * S
    chunk = math.gcd(n, TOKEN_CHUNK)
    y = lax.map(fn, x.reshape(n // chunk, chunk, D))
    return y.reshape(B, S, D)


def _trunk(x, w):
    for layer in range(DEPTH):
        lp = {
            'rel_bias': w['rel_bias'],
            'norm_mix': w['norm_mix'][layer],
            'w_in': w['w_in'][layer],
            'ssm_lambda_re': w['ssm_lambda_re'][layer],
            'ssm_lambda_im': w['ssm_lambda_im'][layer],
            'ssm_log_step': w['ssm_log_step'][layer],
            'ssm_b_re': w['ssm_b_re'][layer],
            'ssm_b_im': w['ssm_b_im'][layer],
            'ssm_c_re': w['ssm_c_re'][layer],
            'ssm_c_im': w['ssm_c_im'][layer],
            'ssm_d': w['ssm_d'][layer],
            'glu_w': w['glu_w'][layer],
            'glu_b': w['glu_b'][layer],
            'attn_out_norm': w['attn_out_norm'][layer],
            'ssm_out_norm': w['ssm_out_norm'][layer],
            'w_out': w['w_out'][layer],
        }
        x = x + lax.map(functools.partial(_token_mixer, lp=lp), x)
        g = w['norm_ffn'][layer]
        j = layer // 2
        if layer % 2 == 0:
            fn = functools.partial(_dense_swiglu, g=g, w1=w['ffn_w1'][j],
                                   w3=w['ffn_w3'][j], w2=w['ffn_w2'][j])
        else:
            fn = functools.partial(_moe_swiglu, g=g, router=w['moe_router'][j],
                                   w1=w['moe_w1'][j], w3=w['moe_w3'][j], w2=w['moe_w2'][j])
        x = x + _token_chunks(fn, x)
    return _rmsnorm(x, w['final_norm'])


def setup_inputs(seed: int = 0) -> dict:
    key = jax.random.key(seed)
    ks = iter(jax.random.split(key, 40))
    f32 = jnp.float32
    G, N, C = N_SSM_GROUPS, SSM_STATE, SSM_GROUP
    n_dense = (DEPTH + 1) // 2
    n_moe = DEPTH // 2

    def nrm(shape, scale):
        return jax.random.normal(next(ks), shape, f32) * scale

    def gain(shape):
        return 1.0 + 0.02 * jax.random.normal(next(ks), shape, f32)

    x_prompt = jax.random.normal(next(ks), (BATCH, SEQ, D_MODEL), f32)
    x_sample = jax.random.normal(next(ks), (DEC_BATCH, DEC_SEQ, D_MODEL), f32)
    n_idx = jnp.arange(SSM_STATE, dtype=f32)
    lam_re = -0.5 + nrm((DEPTH, 2, G, N), 0.01)
    lam_im = jnp.pi * n_idx + nrm((DEPTH, 2, G, N), 0.01)
    lo, hi = math.log(1e-3), math.log(1e-1)
    log_step = lo + jax.random.uniform(next(ks), (DEPTH, 2, G), f32) * (hi - lo)
    return {
        'x_prompt': x_prompt,
        'x_sample': x_sample,
        'rel_bias': nrm((N_REL_BUCKETS, N_ATTN_HEADS), 0.5),
        'norm_mix': gain((DEPTH, D_MODEL)),
        'w_in': nrm((DEPTH, D_MODEL, IN_WIDTH), D_MODEL ** -0.5),
        'ssm_lambda_re': lam_re,
        'ssm_lambda_im': lam_im,
        'ssm_log_step': log_step,
        'ssm_b_re': nrm((DEPTH, 2, G, N, C), (2.0 * C) ** -0.5),
        'ssm_b_im': nrm((DEPTH, 2, G, N, C), (2.0 * C) ** -0.5),
        'ssm_c_re': nrm((DEPTH, 2, G, C, N), (2.0 * N) ** -0.5),
        'ssm_c_im': nrm((DEPTH, 2, G, C, N), (2.0 * N) ** -0.5),
        'ssm_d': nrm((DEPTH, SSM_WIDTH), 1.0),
        'glu_w': nrm((DEPTH, SSM_WIDTH, SSM_WIDTH), SSM_WIDTH ** -0.5),
        'glu_b': nrm((DEPTH, SSM_WIDTH), 0.02),
        'attn_out_norm': gain((DEPTH, ATTN_WIDTH)),
        'ssm_out_norm': gain((DEPTH, SSM_WIDTH)),
        'w_out': nrm((DEPTH, D_MODEL, D_MODEL), D_MODEL ** -0.5),
        'norm_ffn': gain((DEPTH, D_MODEL)),
        'ffn_w1': nrm((n_dense, D_MODEL, D_FF), D_MODEL ** -0.5),
        'ffn_w3': nrm((n_dense, D_MODEL, D_FF), D_MODEL ** -0.5),
        'ffn_w2': nrm((n_dense, D_FF, D_MODEL), D_FF ** -0.5),
        'moe_router': nrm((n_moe, D_MODEL, N_EXPERTS), D_MODEL ** -0.5),
        'moe_w1': nrm((n_moe, N_EXPERTS, D_MODEL, D_FF), D_MODEL ** -0.5),
        'moe_w3': nrm((n_moe, N_EXPERTS, D_MODEL, D_FF), D_MODEL ** -0.5),
        'moe_w2': nrm((n_moe, N_EXPERTS, D_FF, D_MODEL), D_FF ** -0.5),
        'final_norm': gain((D_MODEL,)),
    }


def reference(x_prompt, x_sample, rel_bias, norm_mix, w_in, ssm_lambda_re, ssm_lambda_im,
              ssm_log_step, ssm_b_re, ssm_b_im, ssm_c_re, ssm_c_im, ssm_d, glu_w, glu_b,
              attn_out_norm, ssm_out_norm, w_out, norm_ffn, ffn_w1, ffn_w3, ffn_w2,
              moe_router, moe_w1, moe_w3, moe_w2, final_norm):
    w = {
        'rel_bias': rel_bias, 'norm_mix': norm_mix, 'w_in': w_in,
        'ssm_lambda_re': ssm_lambda_re, 'ssm_lambda_im': ssm_lambda_im,
        'ssm_log_step': ssm_log_step, 'ssm_b_re': ssm_b_re, 'ssm_b_im': ssm_b_im,
        'ssm_c_re': ssm_c_re, 'ssm_c_im': ssm_c_im, 'ssm_d': ssm_d,
        'glu_w': glu_w, 'glu_b': glu_b, 'attn_out_norm': attn_out_norm,
        'ssm_out_norm': ssm_out_norm, 'w_out': w_out, 'norm_ffn': norm_ffn,
        'ffn_w1': ffn_w1, 'ffn_w3': ffn_w3, 'ffn_w2': ffn_w2,
        'moe_router': moe_router, 'moe_w1': moe_w1, 'moe_w3': moe_w3, 'moe_w2': moe_w2,
        'final_norm': final_norm,
    }
    y_prompt = _trunk(x_prompt, w)
    y_sample = _trunk(x_sample, w)
    return (y_prompt, y_sample)
```

```python
import functools
import math

import numpy as np
import jax
import jax.numpy as jnp
from jax import lax
from jax.experimental import pallas as pl
from jax.experimental.pallas import tpu as pltpu

F32 = jnp.float32
BF16 = jnp.bfloat16

LANES = 128
VMEM_LIMIT_BYTES = 56 * 1024 * 1024

HEAD_DIM = 64
HEADS_PER_BLOCK = LANES // HEAD_DIM
DILATIONS = (1, 4, 16)
BAND_HALF = 64
Q_BLOCK = 128
N_REL_BUCKETS = 32
REL_MAX_DISTANCE = 1024
SSM_GROUP = 16
SSM_STATE = 64
SSM_CHUNK = 16
CHUNK_WIDTH = SSM_CHUNK * SSM_GROUP
STATE_WIDTH = 2 * SSM_STATE
TOP_K = 2
NORM_EPS = 1e-6
MASK_VALUE = -1e30

TOKEN_TILE = 512
MOE_TILE = 256


def _cparams(semantics):
    return pltpu.CompilerParams(dimension_semantics=semantics, vmem_limit_bytes=VMEM_LIMIT_BYTES)


def _rms(x, g):
    return x * lax.rsqrt(jnp.mean(x * x, axis=-1, keepdims=True) + NORM_EPS) * g


def _sigmoid(x):
    return 1.0 / (1.0 + jnp.exp(-x))


def _inproj_kernel(x_ref, g_ref, w_ref, q_ref, k_ref, v_ref, u_ref, *, aw):
    h = _rms(x_ref[...], g_ref[...]).astype(BF16)
    scale = HEAD_DIM ** -0.5
    q_ref[...] = (jnp.dot(h, w_ref[:, 0:aw], preferred_element_type=F32) * scale).astype(BF16)
    k_ref[...] = jnp.dot(h, w_ref[:, aw:2 * aw], preferred_element_type=F32).astype(BF16)
    v_ref[...] = jnp.dot(h, w_ref[:, 2 * aw:3 * aw], preferred_element_type=F32).astype(BF16)
    u_ref[...] = jnp.dot(h, w_ref[:, 3 * aw:], preferred_element_type=F32)


def _inproj(x, g, w_bf16, aw):
    t, d = x.shape
    n = w_bf16.shape[1]
    sw = n - 3 * aw
    tm = TOKEN_TILE
    row = lambda i: (i, 0)
    fixed = lambda i: (0, 0)
    return pl.pallas_call(
        functools.partial(_inproj_kernel, aw=aw),
        grid=(t // tm,),
        in_specs=[pl.BlockSpec((tm, d), row), pl.BlockSpec((1, d), fixed), pl.BlockSpec((d, n), fixed)],
        out_specs=[pl.BlockSpec((tm, aw), row), pl.BlockSpec((tm, aw), row),
                   pl.BlockSpec((tm, aw), row), pl.BlockSpec((tm, sw), row)],
        out_shape=[jax.ShapeDtypeStruct((t, aw), BF16)] * 3 + [jax.ShapeDtypeStruct((t, sw), F32)],
        compiler_params=_cparams(("parallel",)),
        name="inproj",
    )(x, g.reshape(1, d), w_bf16)


def _t5_bucket(rel):
    nb = N_REL_BUCKETS // 2
    max_exact = nb // 2
    ret = (rel > 0).astype(np.int32) * nb
    n = np.abs(rel)
    large = max_exact + (np.log(np.maximum(n, 1) / max_exact)
                         / np.log(REL_MAX_DISTANCE / max_exact) * (nb - max_exact)).astype(np.int32)
    large = np.minimum(large, nb - 1)
    return (ret + np.where(n < max_exact, n, large)).astype(np.int32)


def _key_window(sub_len):
    return min(2 * Q_BLOCK, sub_len)


def _bias_tiles(rel_bias, dilation, sub_len):
    w = _key_window(sub_len)
    offsets = (0, -BAND_HALF, -2 * BAND_HALF) if sub_len > Q_BLOCK else (0,)
    i = np.arange(Q_BLOCK)[:, None]
    j = np.arange(w)[None, :]
    tiles = []
    for off in offsets:
        rel = j + off - i
        bias = rel_bias[jnp.asarray(_t5_bucket(rel * dilation))].astype(F32)
        bias = jnp.where(jnp.asarray(np.abs(rel) <= BAND_HALF)[..., None], bias, MASK_VALUE)
        tiles.append(jnp.transpose(bias, (2, 0, 1)))
    return jnp.stack(tiles)


def _attn_tile(qb, kb, vb, bias0, bias1):
    lane = lax.broadcasted_iota(jnp.int32, qb.shape, 1)
    parts = []
    for head, bias in enumerate((bias0, bias1)):
        in_head = (lane < HEAD_DIM) if head == 0 else (lane >= HEAD_DIM)
        qh = jnp.where(in_head, qb, jnp.zeros_like(qb))
        s = lax.dot_general(qh, kb, (((1,), (1,)), ((), ())), preferred_element_type=F32) + bias
        m = jnp.max(s, axis=1, keepdims=True)
        p = jnp.exp(s - m)
        l = jnp.sum(p, axis=1, keepdims=True)
        o = jnp.dot(p.astype(BF16), vb, preferred_element_type=F32)
        parts.append((o, m, l))
    first = lax.broadcasted_iota(jnp.int32, (Q_BLOCK, LANES), 1) < HEAD_DIM
    o = jnp.where(first, parts[0][0], parts[1][0])
    m = jnp.where(first, parts[0][1], parts[1][1])
    l = jnp.where(first, parts[0][2], parts[1][2])
    return o, m, l


def _attn_kernel(q_ref, k_ref, v_ref, b1_ref, b4_ref, b16_ref, o_ref,
                 qf, kf, vf, qd, kd, vd, m_acc, l_acc, *, seq):
    copy_rows = min(512, seq)

    def widen(c, carry):
        rows = pl.ds(pl.multiple_of(c * copy_rows, copy_rows), copy_rows)
        qf[rows, :] = q_ref[rows, :].astype(F32)
        kf[rows, :] = k_ref[rows, :].astype(F32)
        vf[rows, :] = v_ref[rows, :].astype(F32)
        return carry

    lax.fori_loop(0, seq // copy_rows, widen, 0)

    for dil, bias_ref in zip(DILATIONS, (b1_ref, b4_ref, b16_ref)):
        sub_len = seq // dil
        n_blocks = sub_len // Q_BLOCK
        w = _key_window(sub_len)

        def residue(r, carry, dil=dil, bias_ref=bias_ref, sub_len=sub_len, n_blocks=n_blocks, w=w):
            if dil == 1:
                qs, ks, vs = q_ref, k_ref, v_ref
            else:
                rows_per = min(512, sub_len)

                def gather_rows(c, inner):
                    src = pl.ds(r + dil * c * rows_per, rows_per, stride=dil)
                    dst = pl.ds(pl.multiple_of(c * rows_per, rows_per), rows_per)
                    qd[dst, :] = qf[src, :].astype(BF16)
                    kd[dst, :] = kf[src, :].astype(BF16)
                    vd[dst, :] = vf[src, :].astype(BF16)
                    return inner

                lax.fori_loop(0, sub_len // rows_per, gather_rows, 0)
                qs, ks, vs = qd, kd, vd

            def block(i, inner):
                l0 = pl.multiple_of(i * Q_BLOCK, Q_BLOCK)
                if n_blocks == 1:
                    ws, var = 0, 0
                else:
                    ws = pl.multiple_of(jnp.clip(l0 - BAND_HALF, 0, sub_len - w), BAND_HALF)
                    var = jnp.where(i == 0, 0, jnp.where(i == n_blocks - 1, 2, 1))
                qb = qs[pl.ds(l0, Q_BLOCK), :]
                kb = ks[pl.ds(ws, w), :]
                vb = vs[pl.ds(ws, w), :]
                o, m, l = _attn_tile(qb, kb, vb, bias_ref[var, 0], bias_ref[var, 1])
                if dil == 1:
                    rows = pl.ds(l0, Q_BLOCK)
                    o_ref[rows, :] = o
                    m_acc[rows, :] = m
                    l_acc[rows, :] = l
                else:
                    rows = pl.ds(r + dil * l0, Q_BLOCK, stride=dil)
                    m_old = m_acc[rows, :]
                    m_new = jnp.maximum(m_old, m)
                    a = jnp.exp(m_old - m_new)
                    b = jnp.exp(m - m_new)
                    o_ref[rows, :] = a * o_ref[rows, :] + b * o
                    l_acc[rows, :] = a * l_acc[rows, :] + b * l
                    m_acc[rows, :] = m_new
                return inner

            lax.fori_loop(0, n_blocks, block, 0)
            return carry

        if dil == 1:
            residue(0, 0)
        else:
            lax.fori_loop(0, dil, residue, 0)

    def finish(c, carry):
        rows = pl.ds(pl.multiple_of(c * copy_rows, copy_rows), copy_rows)
        o_ref[rows, :] = o_ref[rows, :] / l_acc[rows, :]
        return carry

    lax.fori_loop(0, seq // copy_rows, finish, 0)


def _attention(q, k, v, bias_tiles):
    b, s, a = q.shape
    n_pairs = a // LANES
    assert s % (DILATIONS[-1] * Q_BLOCK) == 0
    seq_spec = pl.BlockSpec((None, s, LANES), lambda bi, hp: (bi, 0, hp))
    bias_specs = [pl.BlockSpec((t.shape[0], HEADS_PER_BLOCK) + t.shape[2:], lambda bi, hp: (0, hp, 0, 0))
                  for t in bias_tiles]
    strided_len = s // DILATIONS[1]
    return pl.pallas_call(
        functools.partial(_attn_kernel, seq=s),
        grid=(b, n_pairs),
        in_specs=[seq_spec, seq_spec, seq_spec] + bias_specs,
        out_specs=seq_spec,
        out_shape=jax.ShapeDtypeStruct((b, s, a), F32),
        scratch_shapes=[pltpu.VMEM((s, LANES), F32)] * 3
        + [pltpu.VMEM((strided_len, LANES), BF16)] * 3
        + [pltpu.VMEM((s, LANES), F32)] * 2,
        compiler_params=_cparams(("parallel", "parallel")),
        name="dilated_attn",
    )(q, k, v, *bias_tiles)


def _ssm_matrices(lam_re, lam_im, log_step, b_re, b_im, c_re, c_im):
    hp = lax.Precision.HIGHEST
    t_len, n_ch = SSM_CHUNK, SSM_GROUP
    step = jnp.exp(log_step.astype(F32))[..., None]
    lr, li = lam_re.astype(F32), lam_im.astype(F32)
    zr, zi = lr * step, li * step
    tau = jnp.arange(t_len + 1, dtype=F32)[:, None, None, None]
    mag = jnp.exp(tau * zr)
    pr, pi = mag * jnp.cos(tau * zi), mag * jnp.sin(tau * zi)
    nr, ni = pr[1] - 1.0, pi[1]
    den = lr * lr + li * li
    fr, fi = (nr * lr + ni * li) / den, (ni * lr - nr * li) / den
    bbr = fr[..., None] * b_re - fi[..., None] * b_im
    bbi = fr[..., None] * b_im + fi[..., None] * b_re
    cpr = c_re[None] * pr[:, :, :, None, :] - c_im[None] * pi[:, :, :, None, :]
    cpi = c_re[None] * pi[:, :, :, None, :] + c_im[None] * pr[:, :, :, None, :]
    kern = (jnp.einsum('tdgcn,dgnk->tdgck', cpr, bbr, precision=hp)
            - jnp.einsum('tdgcn,dgnk->tdgck', cpi, bbi, precision=hp))
    t_idx = np.arange(t_len)
    lag = t_idx[None, :] - t_idx[:, None]
    g = lam_re.shape[1]

    def toeplitz(k_dir, lag_st):
        m = k_dir[jnp.asarray(np.clip(lag_st, 0, t_len))]
        m = jnp.where(jnp.asarray(lag_st >= 0)[:, :, None, None, None], m, 0.0)
        return jnp.transpose(m, (2, 0, 4, 1, 3)).reshape(g, CHUNK_WIDTH, CHUNK_WIDTH)

    m_sum = toeplitz(kern[:, 0], lag) + toeplitz(kern[:, 1], -lag)

    def state_in(direction, powers):
        p_r, p_i = pr[jnp.asarray(powers), direction], pi[jnp.asarray(powers), direction]
        re = p_r[..., None] * bbr[direction][None] - p_i[..., None] * bbi[direction][None]
        im = p_r[..., None] * bbi[direction][None] + p_i[..., None] * bbr[direction][None]
        re = jnp.transpose(re, (1, 0, 3, 2)).reshape(g, CHUNK_WIDTH, SSM_STATE)
        im = jnp.transpose(im, (1, 0, 3, 2)).reshape(g, CHUNK_WIDTH, SSM_STATE)
        return jnp.concatenate([re, im, im, re], axis=-1)

    w1 = jnp.concatenate([m_sum, state_in(0, t_len - 1 - t_idx), state_in(1, t_idx)], axis=-1)

    def state_out(direction, powers):
        c_r = cpr[jnp.asarray(powers), direction]
        c_i = cpi[jnp.asarray(powers), direction]
        re = jnp.transpose(c_r, (1, 3, 0, 2)).reshape(g, SSM_STATE, CHUNK_WIDTH)
        im = jnp.transpose(-c_i, (1, 3, 0, 2)).reshape(g, SSM_STATE, CHUNK_WIDTH)
        return jnp.concatenate([re, im], axis=1)

    w2 = jnp.concatenate([state_out(0, t_idx + 1), state_out(1, t_len - t_idx)], axis=1)
    ar, ai = pr[t_len], pi[t_len]
    coef = jnp.stack([jnp.concatenate([ar[0], ar[0]], -1), jnp.concatenate([-ai[0], ai[0]], -1),
                      jnp.concatenate([ar[1], ar[1]], -1), jnp.concatenate([-ai[1], ai[1]], -1)], axis=1)
    return w1.astype(BF16), w2.astype(BF16), coef


def _ssm_kernel(u_ref, w1_ref, w2_ref, coef_ref, d_ref, y_ref, st_ref, carry_ref, *, n_chunks, batch):
    rows_total = n_chunks * batch
    rc = min(512, rows_total)
    sw = STATE_WIDTH

    def local(c, carry):
        rows = pl.ds(pl.multiple_of(c * rc, rc), rc)
        u = u_ref[rows, :]
        res = jnp.dot(u.astype(BF16), w1_ref[...], preferred_element_type=F32)
        y_ref[rows, :] = res[:, :CHUNK_WIDTH] + u * d_ref[...]
        st_ref[rows, :] = res[:, CHUNK_WIDTH:]
        return carry

    lax.fori_loop(0, rows_total // rc, local, 0)

    ar_f = jnp.broadcast_to(coef_ref[0:1, :], (batch, sw))
    ai_f = jnp.broadcast_to(coef_ref[1:2, :], (batch, sw))
    ar_b = jnp.broadcast_to(coef_ref[2:3, :], (batch, sw))
    ai_b = jnp.broadcast_to(coef_ref[3:4, :], (batch, sw))

    def step(k, state):
        zf, wf, zb, wb = state
        rf = pl.ds(pl.multiple_of(k * batch, batch), batch)
        rb = pl.ds(pl.multiple_of((n_chunks - 1 - k) * batch, batch), batch)
        carry_ref[rf, 0:sw] = zf
        carry_ref[rb, sw:2 * sw] = zb
        zf_n = ar_f * zf + ai_f * wf + st_ref[rf, 0:sw]
        wf_n = ar_f * wf - ai_f * zf + st_ref[rf, sw:2 * sw]
        zb_n = ar_b * zb + ai_b * wb + st_ref[rb, 2 * sw:3 * sw]
        wb_n = ar_b * wb - ai_b * zb + st_ref[rb, 3 * sw:4 * sw]
        return zf_n, wf_n, zb_n, wb_n

    zero = jnp.zeros((batch, sw), F32)
    lax.fori_loop(0, n_chunks, step, (zero, zero, zero, zero), unroll=4)

    def carried(c, carry):
        rows = pl.ds(pl.multiple_of(c * rc, rc), rc)
        y_ref[rows, :] += jnp.dot(carry_ref[rows, :].astype(BF16), w2_ref[...], preferred_element_type=F32)
        return carry

    lax.fori_loop(0, rows_total // rc, carried, 0)


def _ssm_scan(u, w1, w2, coef, d_skip):
    b, s, width = u.shape
    g = width // SSM_GROUP
    n_chunks = s // SSM_CHUNK
    assert b % 8 == 0 and s % SSM_CHUNK == 0
    rows = n_chunks * b
    uc = u.reshape(b, n_chunks, SSM_CHUNK, g, SSM_GROUP).transpose(3, 1, 0, 2, 4).reshape(g, rows, CHUNK_WIDTH)
    d_rows = jnp.tile(d_skip.astype(F32).reshape(g, 1, SSM_GROUP), (1, SSM_CHUNK, 1)).reshape(g, 1, CHUNK_WIDTH)
    per_group = lambda shape: pl.BlockSpec((None,) + shape, lambda gi: (gi,) + (0,) * len(shape))
    yc = pl.pallas_call(
        functools.partial(_ssm_kernel, n_chunks=n_chunks, batch=b),
        grid=(g,),
        in_specs=[per_group((rows, CHUNK_WIDTH)), per_group(w1.shape[1:]), per_group(w2.shape[1:]),
                  per_group(coef.shape[1:]), per_group((1, CHUNK_WIDTH))],
        out_specs=per_group((rows, CHUNK_WIDTH)),
        out_shape=jax.ShapeDtypeStruct((g, rows, CHUNK_WIDTH), F32),
        scratch_shapes=[pltpu.VMEM((rows, 4 * STATE_WIDTH), F32), pltpu.VMEM((rows, 2 * STATE_WIDTH), F32)],
        compiler_params=_cparams(("parallel",)),
        name="s5_scan",
    )(uc, w1, w2, coef, d_rows)
    return yc.reshape(g, n_chunks, b, SSM_CHUNK, SSM_GROUP).transpose(2, 1, 3, 0, 4).reshape(b, s, width)


def _mixout_kernel(x_ref, a_ref, y_ref, gw_ref, gb_ref, ga_ref, gs_ref, wo_ref, o_ref, *, aw):
    y = y_ref[...]
    yg = 0.5 * y * (1.0 + jnp.tanh(math.sqrt(2.0 / math.pi) * (y + 0.044715 * (y * y * y))))
    gate = _sigmoid(jnp.dot(yg.astype(BF16), gw_ref[...], preferred_element_type=F32) + gb_ref[...])
    an = _rms(a_ref[...], ga_ref[...]).astype(BF16)
    sn = _rms(yg * gate, gs_ref[...]).astype(BF16)
    o_ref[...] = (x_ref[...] + jnp.dot(an, wo_ref[0:aw, :], preferred_element_type=F32)
                  + jnp.dot(sn, wo_ref[aw:, :], preferred_element_type=F32))


def _mixout(x, attn, y, glu_w, glu_b, g_attn, g_ssm, w_out):
    t, d = x.shape
    aw, sw = attn.shape[1], y.shape[1]
    tm = TOKEN_TILE
    row = lambda i: (i, 0)
    fixed = lambda i: (0, 0)
    return pl.pallas_call(
        functools.partial(_mixout_kernel, aw=aw),
        grid=(t // tm,),
        in_specs=[pl.BlockSpec((tm, d), row), pl.BlockSpec((tm, aw), row), pl.BlockSpec((tm, sw), row),
                  pl.BlockSpec((sw, sw), fixed), pl.BlockSpec((1, sw), fixed), pl.BlockSpec((1, aw), fixed),
                  pl.BlockSpec((1, sw), fixed), pl.BlockSpec((d, d), fixed)],
        out_specs=pl.BlockSpec((tm, d), row),
        out_shape=jax.ShapeDtypeStruct((t, d), F32),
        compiler_params=_cparams(("parallel",)),
        name="mixer_out",
    )(x, attn, y, glu_w, glu_b.reshape(1, sw), g_attn.reshape(1, aw), g_ssm.reshape(1, sw), w_out)


MXU_WIDTH = 256
FF_CHUNK_TILES = 6


def _ff_chunks(d_ff):
    assert d_ff % MXU_WIDTH == 0
    tiles = d_ff // MXU_WIDTH
    n = -(-tiles // FF_CHUNK_TILES)
    sizes = [(tiles // n + (1 if c < tiles % n else 0)) * MXU_WIDTH for c in range(n)]
    return [(sum(sizes[:c]), sizes[c]) for c in range(n)]


def _swiglu(h, w1_ref, w3_ref, w2_ref):
    d_ff = w1_ref.shape[1]
    acc = None
    for start, size in _ff_chunks(d_ff):
        a = jnp.dot(h, w1_ref[:, start:start + size], preferred_element_type=F32)
        b = jnp.dot(h, w3_ref[:, start:start + size], preferred_element_type=F32)
        act = (a * _sigmoid(a) * b).astype(BF16)
        part = jnp.dot(act, w2_ref[start:start + size, :], preferred_element_type=F32)
        acc = part if acc is None else acc + part
    return acc


def _ffn_kernel(x_ref, g_ref, w1_ref, w3_ref, w2_ref, o_ref):
    x = x_ref[...]
    h = _rms(x, g_ref[...]).astype(BF16)
    o_ref[...] = x + _swiglu(h, w1_ref, w3_ref, w2_ref)


def _dense_ffn(x, g, w1, w3, w2):
    t, d = x.shape
    d_ff = w1.shape[1]
    tm = TOKEN_TILE
    row = lambda i: (i, 0)
    fixed = lambda i: (0, 0)
    resident = lambda shape: pl.BlockSpec(shape, fixed, pipeline_mode=pl.Buffered(1))
    return pl.pallas_call(
        _ffn_kernel,
        grid=(t // tm,),
        in_specs=[pl.BlockSpec((tm, d), row), pl.BlockSpec((1, d), fixed),
                  resident((d, d_ff)), resident((d, d_ff)), resident((d_ff, d))],
        out_specs=pl.BlockSpec((tm, d), row),
        out_shape=jax.ShapeDtypeStruct((t, d), F32),
        compiler_params=_cparams(("parallel",)),
        name="dense_ffn",
    )(x, g.reshape(1, d), w1, w3, w2)


def _router_kernel(x_ref, g_ref, r_ref, idx_ref, wt_ref, *, n_experts):
    h = _rms(x_ref[...], g_ref[...])
    logits = jnp.dot(h, r_ref[...], preferred_element_type=F32, precision=lax.Precision.HIGHEST)
    lane = lax.broadcasted_iota(jnp.int32, logits.shape, 1).astype(F32)
    logits = jnp.where(lane < n_experts, logits, -jnp.inf)
    v1 = jnp.max(logits, axis=1, keepdims=True)
    i1 = jnp.min(jnp.where(logits == v1, lane, float(LANES)), axis=1, keepdims=True)
    rest = jnp.where(lane == i1, -jnp.inf, logits)
    v2 = jnp.max(rest, axis=1, keepdims=True)
    i2 = jnp.min(jnp.where(rest == v2, lane, float(LANES)), axis=1, keepdims=True)
    e = jnp.exp(v2 - v1)
    w1 = 1.0 / (1.0 + e)
    w2 = e / (1.0 + e)
    idx_ref[...] = jnp.where(lane == 0.0, i1, jnp.where(lane == 1.0, i2, 0.0))[:, :8].astype(jnp.int32)
    wt_ref[...] = jnp.where(lane == 0.0, w1, jnp.where(lane == 1.0, w2, 0.0))[:, :8]


def _router(x, g, router):
    t, d = x.shape
    n_experts = router.shape[1]
    tm = TOKEN_TILE
    r_pad = jnp.zeros((d, LANES), F32).at[:, :n_experts].set(router.astype(F32))
    row = lambda i: (i, 0)
    fixed = lambda i: (0, 0)
    return pl.pallas_call(
        functools.partial(_router_kernel, n_experts=n_experts),
        grid=(t // tm,),
        in_specs=[pl.BlockSpec((tm, d), row), pl.BlockSpec((1, d), fixed), pl.BlockSpec((d, LANES), fixed)],
        out_specs=[pl.BlockSpec((tm, 8), row), pl.BlockSpec((tm, 8), row)],
        out_shape=[jax.ShapeDtypeStruct((t, 8), jnp.int32), jax.ShapeDtypeStruct((t, 8), F32)],
        compiler_params=_cparams(("parallel",)),
        name="router",
    )(x, g.reshape(1, d), r_pad)


def _row_copy(src, src_row, dst, dst_row, sem):
    return pltpu.make_async_copy(src.at[pl.ds(src_row, 1), :], dst.at[pl.ds(dst_row, 1), :], sem)


def _moe_kernel(te_ref, src0_ref, srcn_ref, dst_ref, x_hbm, g_ref, w1_ref, w3_ref, w2_ref, o_hbm,
                xbuf, ybuf, gsem, ssem, *, tm):
    del te_ref
    i = pl.program_id(0)
    n_tiles = pl.num_programs(0)
    slot = lax.rem(i, 2)
    other = 1 - slot

    def start_gather(idx_ref, s):
        for j in range(tm):
            _row_copy(x_hbm, idx_ref[0, j], xbuf.at[s], j, gsem.at[s]).start()

    def wait_gather(s):
        for j in range(tm):
            _row_copy(x_hbm, 0, xbuf.at[s], j, gsem.at[s]).wait()

    def wait_scatter(s):
        for j in range(tm):
            _row_copy(ybuf.at[s], j, o_hbm, 0, ssem.at[s]).wait()

    @pl.when(i == 0)
    def _():
        start_gather(src0_ref, 0)

    @pl.when(i >= 2)
    def _():
        wait_scatter(slot)

    wait_gather(slot)
    start_gather(srcn_ref, other)
    h = _rms(xbuf[slot], g_ref[...]).astype(BF16)
    ybuf[slot] = _swiglu(h, w1_ref, w3_ref, w2_ref)
    for j in range(tm):
        _row_copy(ybuf.at[slot], j, o_hbm, dst_ref[0, j], ssem.at[slot]).start()

    @pl.when(i == n_tiles - 1)
    def _():
        wait_gather(other)
        wait_scatter(slot)

        @pl.when(n_tiles >= 2)
        def _():
            wait_scatter(other)


def _moe_plan(idx, n_experts, tm):
    t = idx.shape[0]
    pad = n_experts * tm
    n_rows = TOP_K * t + pad
    n_tiles = n_rows // tm
    flat = idx.reshape(-1)
    order = jnp.argsort(flat, stable=True).astype(jnp.int32)
    sorted_e = flat[order]
    counts = jnp.zeros((n_experts,), jnp.int32).at[flat].add(1)
    padded = (counts + tm - 1) // tm * tm
    off = jnp.cumsum(counts) - counts
    poff = jnp.cumsum(padded) - padded
    pos = poff[sorted_e] + (jnp.arange(TOP_K * t, dtype=jnp.int32) - off[sorted_e])
    token = order // TOP_K
    slot = order % TOP_K
    src = jnp.zeros((n_rows,), jnp.int32).at[pos].set(token)
    is_pad = jnp.ones((n_rows,), jnp.int32).at[pos].set(0)
    scratch_row = TOP_K * t + jnp.cumsum(is_pad) - is_pad
    dst = scratch_row.astype(jnp.int32).at[pos].set(slot * t + token)
    ends = jnp.cumsum(padded)
    tile_e = jnp.searchsorted(ends, jnp.arange(n_tiles, dtype=jnp.int32) * tm, side='right').astype(jnp.int32)
    last_valid = jnp.maximum(ends[-1] // tm - 1, 0)
    tile_e = jnp.minimum(tile_e, tile_e[last_valid])
    return src.reshape(n_tiles, 1, tm), dst.reshape(n_tiles, 1, tm), tile_e


def _moe_experts(x, g, w1, w3, w2, src, dst, tile_e):
    t, d = x.shape
    d_ff = w1.shape[2]
    n_tiles, _, tm = src.shape
    smem_row = lambda fn: pl.BlockSpec((None, 1, tm), fn, memory_space=pltpu.SMEM)
    expert = lambda shape: pl.BlockSpec((None,) + shape, lambda i, te: (te[i], 0, 0))
    grid_spec = pltpu.PrefetchScalarGridSpec(
        num_scalar_prefetch=1,
        grid=(n_tiles,),
        in_specs=[smem_row(lambda i, te: (0, 0, 0)),
                  smem_row(lambda i, te: (jnp.minimum(i + 1, n_tiles - 1), 0, 0)),
                  smem_row(lambda i, te: (i, 0, 0)),
                  pl.BlockSpec(memory_space=pl.ANY),
                  pl.BlockSpec((1, d), lambda i, te: (0, 0)),
                  expert((d, d_ff)), expert((d, d_ff)), expert((d_ff, d))],
        out_specs=pl.BlockSpec(memory_space=pl.ANY),
        scratch_shapes=[pltpu.VMEM((2, tm, d), F32), pltpu.VMEM((2, tm, d), F32),
                        pltpu.SemaphoreType.DMA((2,)), pltpu.SemaphoreType.DMA((2,))],
    )
    return pl.pallas_call(
        functools.partial(_moe_kernel, tm=tm),
        grid_spec=grid_spec,
        out_shape=jax.ShapeDtypeStruct((n_tiles * tm, d), F32),
        compiler_params=_cparams(("arbitrary",)),
        name="moe_experts",
    )(tile_e, src, src, dst, x, g.reshape(1, d), w1, w3, w2)


def _combine_kernel(x_ref, y0_ref, y1_ref, wt_ref, g_ref, o_ref, *, final_norm):
    wt = wt_ref[...]
    out = x_ref[...] + wt[:, 0:1] * y0_ref[...] + wt[:, 1:2] * y1_ref[...]
    if final_norm:
        out = _rms(out, g_ref[...])
    o_ref[...] = out


def _moe_combine(x, y2, wt, g_final, final_norm):
    t, d = x.shape
    tm = TOKEN_TILE
    row = lambda i: (i, 0)
    return pl.pallas_call(
        functools.partial(_combine_kernel, final_norm=final_norm),
        grid=(t // tm,),
        in_specs=[pl.BlockSpec((tm, d), row), pl.BlockSpec((tm, d), row),
                  pl.BlockSpec((tm, d), lambda i: (i + t // tm, 0)),
                  pl.BlockSpec((tm, 8), row), pl.BlockSpec((1, d), lambda i: (0, 0))],
        out_specs=pl.BlockSpec((tm, d), row),
        out_shape=jax.ShapeDtypeStruct((t, d), F32),
        compiler_params=_cparams(("parallel",)),
        name="moe_combine",
    )(x, y2, y2, wt, g_final.reshape(1, d))


def _norm_kernel(x_ref, g_ref, o_ref):
    o_ref[...] = _rms(x_ref[...], g_ref[...])


def _final_norm(x, g):
    t, d = x.shape
    tm = TOKEN_TILE
    return pl.pallas_call(
        _norm_kernel,
        grid=(t // tm,),
        in_specs=[pl.BlockSpec((tm, d), lambda i: (i, 0)), pl.BlockSpec((1, d), lambda i: (0, 0))],
        out_specs=pl.BlockSpec((tm, d), lambda i: (i, 0)),
        out_shape=jax.ShapeDtypeStruct((t, d), F32),
        compiler_params=_cparams(("parallel",)),
        name="final_norm",
    )(x, g.reshape(1, d))


def _trunk(x, p):
    b, s, d = x.shape
    t = b * s
    aw = p['attn_out_norm'].shape[1]
    depth = p['w_in'].shape[0]
    xt = x.reshape(t, d)
    normed = False
    for layer in range(depth):
        q, k, v, u = _inproj(xt, p['norm_mix'][layer], p['w_in'][layer], aw)
        shape3 = lambda z: z.reshape(b, s, z.shape[1])
        attn = _attention(shape3(q), shape3(k), shape3(v), p['bias_tiles'][s]).reshape(t, aw)
        w1m, w2m, coef = p['ssm'][layer]
        y = _ssm_scan(shape3(u), w1m, w2m, coef, p['ssm_d'][layer]).reshape(t, -1)
        xt = _mixout(xt, attn, y, p['glu_w'][layer], p['glu_b'][layer], p['attn_out_norm'][layer],
                     p['ssm_out_norm'][layer], p['w_out'][layer])
        g = p['norm_ffn'][layer]
        j = layer // 2
        if layer % 2 == 0:
            xt = _dense_ffn(xt, g, p['ffn_w1'][j], p['ffn_w3'][j], p['ffn_w2'][j])
        else:
            idx, wt = _router(xt, g, p['moe_router'][j])
            n_experts = p['moe_router'].shape[2]
            src, dst, tile_e = _moe_plan(idx[:, :TOP_K], n_experts, MOE_TILE)
            y2 = _moe_experts(xt, g, p['moe_w1'][j], p['moe_w3'][j], p['moe_w2'][j], src, dst, tile_e)
            last = layer == depth - 1
            xt = _moe_combine(xt, y2, wt, p['final_norm'], last)
            normed = last
    if not normed:
        xt = _final_norm(xt, p['final_norm'])
    return xt.reshape(b, s, d)


def kernel(x_prompt, x_sample, rel_bias, norm_mix, w_in, ssm_lambda_re, ssm_lambda_im, ssm_log_step, ssm_b_re, ssm_b_im, ssm_c_re, ssm_c_im, ssm_d, glu_w, glu_b, attn_out_norm, ssm_out_norm, w_out, norm_ffn, ffn_w1, ffn_w3, ffn_w2, moe_router, moe_w1, moe_w3, moe_w2, final_norm):
    depth = w_in.shape[0]
    bias_tiles = {}
    for x in (x_prompt, x_sample):
        s = x.shape[1]
        if s not in bias_tiles:
            bias_tiles[s] = [_bias_tiles(rel_bias, dil, s // dil) for dil in DILATIONS]
    p = {
        'bias_tiles': bias_tiles,
        'norm_mix': norm_mix, 'w_in': w_in.astype(BF16),
        'ssm': [_ssm_matrices(ssm_lambda_re[l], ssm_lambda_im[l], ssm_log_step[l], ssm_b_re[l], ssm_b_im[l],
                              ssm_c_re[l], ssm_c_im[l]) for l in range(depth)],
        'ssm_d': ssm_d, 'glu_w': glu_w.astype(BF16), 'glu_b': glu_b,
        'attn_out_norm': attn_out_norm, 'ssm_out_norm': ssm_out_norm, 'w_out': w_out.astype(BF16),
        'norm_ffn': norm_ffn,
        'ffn_w1': ffn_w1.astype(BF16), 'ffn_w3': ffn_w3.astype(BF16), 'ffn_w2': ffn_w2.astype(BF16),
        'moe_router': moe_router,
        'moe_w1': moe_w1.astype(BF16), 'moe_w3': moe_w3.astype(BF16), 'moe_w2': moe_w2.astype(BF16),
        'final_norm': final_norm,
    }
    return (_trunk(x_prompt, p), _trunk(x_sample, p))
```

```python
import functools
import math

import numpy as np
import jax
import jax.numpy as jnp
from jax import lax
from jax.experimental import pallas as pl
from jax.experimental.pallas import tpu as pltpu

F32 = jnp.float32
BF16 = jnp.bfloat16

LANES = 128
VMEM_LIMIT_BYTES = 56 * 1024 * 1024

HEAD_DIM = 64
HEADS_PER_BLOCK = LANES // HEAD_DIM
DILATIONS = (1, 4, 16)
BAND_HALF = 64
Q_BLOCK = 128
N_REL_BUCKETS = 32
REL_MAX_DISTANCE = 1024
SSM_GROUP = 16
SSM_STATE = 64
GROUPS_PER_BLOCK = LANES // SSM_GROUP
SSM_CHUNK = 16
CHUNK_WIDTH = SSM_CHUNK * SSM_GROUP
STATE_WIDTH = 2 * SSM_STATE
TOP_K = 2
NORM_EPS = 1e-6
MASK_VALUE = -1e30

TOKEN_TILE = 512
MOE_TILE = 256


def _cparams(semantics):
    return pltpu.CompilerParams(dimension_semantics=semantics, vmem_limit_bytes=VMEM_LIMIT_BYTES)


def _rms(x, g):
    return x * lax.rsqrt(jnp.mean(x * x, axis=-1, keepdims=True) + NORM_EPS) * g


def _sigmoid(x):
    return 1.0 / (1.0 + jnp.exp(-x))


def _granule_transpose(vs):
    lane_granule = lax.broadcasted_iota(jnp.int32, vs[0].shape, 1) // SSM_GROUP
    for dist in (4, 2, 1):
        upper = (lane_granule & dist) != 0
        new = list(vs)
        for i in range(GROUPS_PER_BLOCK):
            if i & dist == 0:
                j = i + dist
                new[i] = jnp.where(upper, pltpu.roll(vs[j], SSM_GROUP * dist, 1), vs[i])
                new[j] = jnp.where(upper, vs[j], pltpu.roll(vs[i], LANES - SSM_GROUP * dist, 1))
        vs = new
    return vs


def _tokens_to_chunks(tok_ref, chunk_ref, n_chunks):
    for blk in range(tok_ref.shape[0]):
        for half in range(CHUNK_WIDTH // LANES):
            by_token = [tok_ref[blk, pl.ds(half * GROUPS_PER_BLOCK + t, n_chunks, stride=SSM_CHUNK), :]
                        for t in range(GROUPS_PER_BLOCK)]
            for g, val in enumerate(_granule_transpose(by_token)):
                chunk_ref[blk * GROUPS_PER_BLOCK + g, :, half * LANES:(half + 1) * LANES] = val


def _chunks_to_tokens(chunk_ref, tok_ref, n_chunks):
    for blk in range(tok_ref.shape[0]):
        for half in range(CHUNK_WIDTH // LANES):
            by_group = [chunk_ref[blk * GROUPS_PER_BLOCK + g, :, half * LANES:(half + 1) * LANES]
                        for g in range(GROUPS_PER_BLOCK)]
            for t, val in enumerate(_granule_transpose(by_group)):
                tok_ref[blk, pl.ds(half * GROUPS_PER_BLOCK + t, n_chunks, stride=SSM_CHUNK), :] = val


def _inproj_kernel(x_ref, g_ref, w_ref, q_ref, k_ref, v_ref, uc_ref, u_tok, *, aw):
    h = _rms(x_ref[...], g_ref[...]).astype(BF16)
    scale = HEAD_DIM ** -0.5
    q_ref[...] = (jnp.dot(h, w_ref[:, 0:aw], preferred_element_type=F32) * scale).astype(BF16)
    k_ref[...] = jnp.dot(h, w_ref[:, aw:2 * aw], preferred_element_type=F32).astype(BF16)
    v_ref[...] = jnp.dot(h, w_ref[:, 2 * aw:3 * aw], preferred_element_type=F32).astype(BF16)
    u = jnp.dot(h, w_ref[:, 3 * aw:], preferred_element_type=F32)
    for blk in range(u_tok.shape[0]):
        u_tok[blk] = u[:, blk * LANES:(blk + 1) * LANES]
    _tokens_to_chunks(u_tok, uc_ref, x_ref.shape[0] // SSM_CHUNK)


def _inproj(x, g, w_bf16, aw):
    t, d = x.shape
    n = w_bf16.shape[1]
    sw = n - 3 * aw
    tm = TOKEN_TILE
    groups, tile_chunks = sw // SSM_GROUP, tm // SSM_CHUNK
    row = lambda i: (i, 0)
    fixed = lambda i: (0, 0)
    return pl.pallas_call(
        functools.partial(_inproj_kernel, aw=aw),
        grid=(t // tm,),
        in_specs=[pl.BlockSpec((tm, d), row), pl.BlockSpec((1, d), fixed), pl.BlockSpec((d, n), fixed)],
        out_specs=[pl.BlockSpec((tm, aw), row), pl.BlockSpec((tm, aw), row), pl.BlockSpec((tm, aw), row),
                   pl.BlockSpec((groups, tile_chunks, CHUNK_WIDTH), lambda i: (0, i, 0))],
        out_shape=[jax.ShapeDtypeStruct((t, aw), BF16)] * 3
        + [jax.ShapeDtypeStruct((groups, t // SSM_CHUNK, CHUNK_WIDTH), F32)],
        scratch_shapes=[pltpu.VMEM((sw // LANES, tm, LANES), F32)],
        compiler_params=_cparams(("parallel",)),
        name="inproj",
    )(x, g.reshape(1, d), w_bf16)


def _t5_bucket(rel):
    nb = N_REL_BUCKETS // 2
    max_exact = nb // 2
    ret = (rel > 0).astype(np.int32) * nb
    n = np.abs(rel)
    large = max_exact + (np.log(np.maximum(n, 1) / max_exact)
                         / np.log(REL_MAX_DISTANCE / max_exact) * (nb - max_exact)).astype(np.int32)
    large = np.minimum(large, nb - 1)
    return (ret + np.where(n < max_exact, n, large)).astype(np.int32)


def _key_window(sub_len):
    return min(2 * Q_BLOCK, sub_len)


def _bias_tiles(rel_bias, dilation, sub_len):
    w = _key_window(sub_len)
    offsets = (0, -BAND_HALF, -2 * BAND_HALF) if sub_len > Q_BLOCK else (0,)
    i = np.arange(Q_BLOCK)[:, None]
    j = np.arange(w)[None, :]
    tiles = []
    for off in offsets:
        rel = j + off - i
        bias = rel_bias[jnp.asarray(_t5_bucket(rel * dilation))].astype(F32)
        bias = jnp.where(jnp.asarray(np.abs(rel) <= BAND_HALF)[..., None], bias, MASK_VALUE)
        tiles.append(jnp.transpose(bias, (2, 0, 1)))
    return jnp.stack(tiles)


def _attn_tile(qb, kb, vb, bias0, bias1):
    lane = lax.broadcasted_iota(jnp.int32, qb.shape, 1)
    parts = []
    for head, bias in enumerate((bias0, bias1)):
        in_head = (lane < HEAD_DIM) if head == 0 else (lane >= HEAD_DIM)
        qh = jnp.where(in_head, qb, jnp.zeros_like(qb))
        s = lax.dot_general(qh, kb, (((1,), (1,)), ((), ())), preferred_element_type=F32) + bias
        m = jnp.max(s, axis=1, keepdims=True)
        p = jnp.exp(s - m)
        l = jnp.sum(p, axis=1, keepdims=True)
        o = jnp.dot(p.astype(BF16), vb, preferred_element_type=F32)
        parts.append((o, m, l))
    first = lax.broadcasted_iota(jnp.int32, (Q_BLOCK, LANES), 1) < HEAD_DIM
    o = jnp.where(first, parts[0][0], parts[1][0])
    m = jnp.where(first, parts[0][1], parts[1][1])
    l = jnp.where(first, parts[0][2], parts[1][2])
    return o, m, l


ATTN_UNROLL = 4
ATTN_STAGE_ROWS = 2048
COPY_ROWS = 512


def _attn_kernel(q_ref, k_ref, v_ref, b1_ref, b4_ref, b16_ref, o_ref,
                 stage, qd, kd, vd, m_acc, l_acc, *, seq):
    for dil, bias_ref in zip(DILATIONS, (b1_ref, b4_ref, b16_ref)):
        sub_len = seq // dil
        n_blocks = sub_len // Q_BLOCK
        assert n_blocks & (n_blocks - 1) == 0 and (dil * n_blocks) % ATTN_UNROLL == 0
        w = _key_window(sub_len)

        if dil == 1:
            qs, ks, vs = q_ref, k_ref, v_ref
        else:
            seg = min(ATTN_STAGE_ROWS, seq)
            per = seg // dil

            def segment(sg, carry, dil=dil, sub_len=sub_len, seg=seg, per=per):
                base = pl.multiple_of(sg * seg, seg)
                for idx, src in enumerate((q_ref, k_ref, v_ref)):
                    for c in range(seg // COPY_ROWS):
                        stage[idx, c * COPY_ROWS:(c + 1) * COPY_ROWS, :] = (
                            src[pl.ds(base + c * COPY_ROWS, COPY_ROWS), :].astype(F32))

                def residue(r, inner):
                    dst = pl.ds(pl.multiple_of(r * sub_len + sg * per, Q_BLOCK), per)
                    for idx, dst_ref in enumerate((qd, kd, vd)):
                        dst_ref[dst, :] = stage[idx, pl.ds(r, per, stride=dil), :].astype(BF16)
                    return inner

                lax.fori_loop(0, dil, residue, 0)
                return carry

            lax.fori_loop(0, seq // seg, segment, 0)
            qs, ks, vs = qd, kd, vd

        def trip(j, carry, dil=dil, bias_ref=bias_ref, sub_len=sub_len, n_blocks=n_blocks, w=w,
                 qs=qs, ks=ks, vs=vs):
            for uu in range(ATTN_UNROLL):
                item = j * ATTN_UNROLL + uu
                if n_blocks == 1:
                    r, l0, ws, var = item, 0, 0, 0
                else:
                    r = jnp.right_shift(item, n_blocks.bit_length() - 1) if dil > 1 else 0
                    i = jnp.bitwise_and(item, n_blocks - 1) if dil > 1 else item
                    l0 = i * Q_BLOCK
                    ws = jnp.clip(l0 - BAND_HALF, 0, sub_len - w)
                    var = jnp.where(i == 0, 0, jnp.where(i == n_blocks - 1, 2, 1))
                base = r * sub_len
                qb = qs[pl.ds(pl.multiple_of(base + l0, Q_BLOCK), Q_BLOCK), :]
                kb = ks[pl.ds(pl.multiple_of(base + ws, BAND_HALF), w), :]
                vb = vs[pl.ds(pl.multiple_of(base + ws, BAND_HALF), w), :]
                o, m, l = _attn_tile(qb, kb, vb, bias_ref[var, 0], bias_ref[var, 1])
                if dil == 1:
                    rows = pl.ds(pl.multiple_of(l0, Q_BLOCK), Q_BLOCK)
                    o_ref[rows, :] = o
                    m_acc[rows, :] = m
                    l_acc[rows, :] = l
                else:
                    rows = pl.ds(r + dil * l0, Q_BLOCK, stride=dil)
                    m_old = m_acc[rows, :]
                    m_new = jnp.maximum(m_old, m)
                    a = jnp.exp(m_old - m_new)
                    b = jnp.exp(m - m_new)
                    o_ref[rows, :] = a * o_ref[rows, :] + b * o
                    l_acc[rows, :] = a * l_acc[rows, :] + b * l
                    m_acc[rows, :] = m_new
            return carry

        lax.fori_loop(0, dil * n_blocks // ATTN_UNROLL, trip, 0)

    def finish(c, carry):
        rows = pl.ds(pl.multiple_of(c * COPY_ROWS, COPY_ROWS), COPY_ROWS)
        o_ref[rows, :] = o_ref[rows, :] / l_acc[rows, :]
        return carry

    lax.fori_loop(0, seq // COPY_ROWS, finish, 0)


def _attention(q, k, v, bias_tiles):
    b, s, a = q.shape
    n_pairs = a // LANES
    assert s % (DILATIONS[-1] * Q_BLOCK) == 0
    seq_spec = pl.BlockSpec((None, s, LANES), lambda bi, hp: (bi, 0, hp))
    bias_specs = [pl.BlockSpec((t.shape[0], HEADS_PER_BLOCK) + t.shape[2:], lambda bi, hp: (0, hp, 0, 0))
                  for t in bias_tiles]
    return pl.pallas_call(
        functools.partial(_attn_kernel, seq=s),
        grid=(b, n_pairs),
        in_specs=[seq_spec, seq_spec, seq_spec] + bias_specs,
        out_specs=seq_spec,
        out_shape=jax.ShapeDtypeStruct((b, s, a), F32),
        scratch_shapes=[pltpu.VMEM((3, min(ATTN_STAGE_ROWS, s), LANES), F32)]
        + [pltpu.VMEM((s, LANES), BF16)] * 3
        + [pltpu.VMEM((s, LANES), F32)] * 2,
        compiler_params=_cparams(("parallel", "parallel")),
        name="dilated_attn",
    )(q, k, v, *bias_tiles)


def _ssm_matrices(lam_re, lam_im, log_step, b_re, b_im, c_re, c_im):
    hp = lax.Precision.HIGHEST
    t_len, n_ch = SSM_CHUNK, SSM_GROUP
    step = jnp.exp(log_step.astype(F32))[..., None]
    lr, li = lam_re.astype(F32), lam_im.astype(F32)
    zr, zi = lr * step, li * step
    tau = jnp.arange(t_len + 1, dtype=F32)[:, None, None, None]
    mag = jnp.exp(tau * zr)
    pr, pi = mag * jnp.cos(tau * zi), mag * jnp.sin(tau * zi)
    nr, ni = pr[1] - 1.0, pi[1]
    den = lr * lr + li * li
    fr, fi = (nr * lr + ni * li) / den, (ni * lr - nr * li) / den
    bbr = fr[..., None] * b_re - fi[..., None] * b_im
    bbi = fr[..., None] * b_im + fi[..., None] * b_re
    cpr = c_re[None] * pr[:, :, :, None, :] - c_im[None] * pi[:, :, :, None, :]
    cpi = c_re[None] * pi[:, :, :, None, :] + c_im[None] * pr[:, :, :, None, :]
    kern = (jnp.einsum('tdgcn,dgnk->tdgck', cpr, bbr, precision=hp)
            - jnp.einsum('tdgcn,dgnk->tdgck', cpi, bbi, precision=hp))
    t_idx = np.arange(t_len)
    lag = t_idx[None, :] - t_idx[:, None]
    g = lam_re.shape[1]

    def toeplitz(k_dir, lag_st):
        m = k_dir[jnp.asarray(np.clip(lag_st, 0, t_len))]
        m = jnp.where(jnp.asarray(lag_st >= 0)[:, :, None, None, None], m, 0.0)
        return jnp.transpose(m, (2, 0, 4, 1, 3)).reshape(g, CHUNK_WIDTH, CHUNK_WIDTH)

    m_sum = toeplitz(kern[:, 0], lag) + toeplitz(kern[:, 1], -lag)

    def state_in(direction, powers):
        p_r, p_i = pr[jnp.asarray(powers), direction], pi[jnp.asarray(powers), direction]
        re = p_r[..., None] * bbr[direction][None] - p_i[..., None] * bbi[direction][None]
        im = p_r[..., None] * bbi[direction][None] + p_i[..., None] * bbr[direction][None]
        re = jnp.transpose(re, (1, 0, 3, 2)).reshape(g, CHUNK_WIDTH, SSM_STATE)
        im = jnp.transpose(im, (1, 0, 3, 2)).reshape(g, CHUNK_WIDTH, SSM_STATE)
        return jnp.concatenate([re, im, im, re], axis=-1)

    w1 = jnp.concatenate([m_sum, state_in(0, t_len - 1 - t_idx), state_in(1, t_idx)], axis=-1)

    def state_out(direction, powers):
        c_r = cpr[jnp.asarray(powers), direction]
        c_i = cpi[jnp.asarray(powers), direction]
        re = jnp.transpose(c_r, (1, 3, 0, 2)).reshape(g, SSM_STATE, CHUNK_WIDTH)
        im = jnp.transpose(-c_i, (1, 3, 0, 2)).reshape(g, SSM_STATE, CHUNK_WIDTH)
        return jnp.concatenate([re, im], axis=1)

    w2 = jnp.concatenate([state_out(0, t_idx + 1), state_out(1, t_len - t_idx)], axis=1)
    ar, ai = pr[t_len], pi[t_len]
    coef = jnp.stack([jnp.concatenate([ar[0], ar[0]], -1), jnp.concatenate([-ai[0], ai[0]], -1),
                      jnp.concatenate([ar[1], ar[1]], -1), jnp.concatenate([-ai[1], ai[1]], -1)], axis=1)
    return w1.astype(BF16), w2.astype(BF16), coef


SUBLANES = 8
SCAN_PITCH_PAD = 8


def _ssm_kernel(u_ref, w1_ref, w2_ref, coef_ref, d_ref, y_ref, zf_loc, wf_loc, zb_loc, wb_loc,
                zf_in, zb_in, *, n_chunks, batch):
    rc = min(512, n_chunks)
    per_seq = n_chunks // rc
    pitch = n_chunks + SCAN_PITCH_PAD
    sw = STATE_WIDTH
    octets = batch // SUBLANES

    def split(c):
        seq, part = (c, 0) if per_seq == 1 else (c // per_seq, c % per_seq)
        return (pl.ds(pl.multiple_of(seq * n_chunks + part * rc, SUBLANES), rc),
                pl.ds(pl.multiple_of(seq * pitch + part * rc, SUBLANES), rc))

    def local(c, carry):
        rows, srows = split(c)
        u = u_ref[rows, :]
        res = jnp.dot(u.astype(BF16), w1_ref[...], preferred_element_type=F32)
        y_ref[rows, :] = res[:, :CHUNK_WIDTH] + u * d_ref[...]
        for idx, dst in enumerate((zf_loc, wf_loc, zb_loc, wb_loc)):
            dst[srows, :] = res[:, CHUNK_WIDTH + idx * sw:CHUNK_WIDTH + (idx + 1) * sw]
        return carry

    lax.fori_loop(0, batch * per_seq, local, 0)

    ar_f = jnp.broadcast_to(coef_ref[0:1, :], (SUBLANES, sw))
    ai_f = jnp.broadcast_to(coef_ref[1:2, :], (SUBLANES, sw))
    ar_b = jnp.broadcast_to(coef_ref[2:3, :], (SUBLANES, sw))
    ai_b = jnp.broadcast_to(coef_ref[3:4, :], (SUBLANES, sw))

    def step(k, state):
        new = []
        for oc in range(octets):
            zf, wf, zb, wb = state[4 * oc:4 * oc + 4]
            rf = pl.ds(oc * SUBLANES * pitch + k, SUBLANES, stride=pitch)
            rb = pl.ds(oc * SUBLANES * pitch + (n_chunks - 1 - k), SUBLANES, stride=pitch)
            zf_in[rf, :] = zf
            zb_in[rb, :] = zb
            new += [ar_f * zf + ai_f * wf + zf_loc[rf, :], ar_f * wf - ai_f * zf + wf_loc[rf, :],
                    ar_b * zb + ai_b * wb + zb_loc[rb, :], ar_b * wb - ai_b * zb + wb_loc[rb, :]]
        return tuple(new)

    zero = jnp.zeros((SUBLANES, sw), F32)
    lax.fori_loop(0, n_chunks, step, (zero,) * (4 * octets), unroll=4)

    def carried(c, carry):
        rows, srows = split(c)
        y_ref[rows, :] += (
            jnp.dot(zf_in[srows, :].astype(BF16), w2_ref[0:sw, :], preferred_element_type=F32)
            + jnp.dot(zb_in[srows, :].astype(BF16), w2_ref[sw:, :], preferred_element_type=F32))
        return carry

    lax.fori_loop(0, batch * per_seq, carried, 0)


def _ssm_scan(uc, w1, w2, coef, d_skip, batch):
    g, rows, _ = uc.shape
    n_chunks = rows // batch
    assert batch % SUBLANES == 0 and n_chunks % SUBLANES == 0
    d_rows = jnp.tile(d_skip.astype(F32).reshape(g, 1, SSM_GROUP), (1, SSM_CHUNK, 1)).reshape(g, 1, CHUNK_WIDTH)
    per_group = lambda shape: pl.BlockSpec((None,) + shape, lambda gi: (gi,) + (0,) * len(shape))
    scratch_rows = batch * (n_chunks + SCAN_PITCH_PAD)
    return pl.pallas_call(
        functools.partial(_ssm_kernel, n_chunks=n_chunks, batch=batch),
        grid=(g,),
        in_specs=[per_group((rows, CHUNK_WIDTH)), per_group(w1.shape[1:]), per_group(w2.shape[1:]),
                  per_group(coef.shape[1:]), per_group((1, CHUNK_WIDTH))],
        out_specs=per_group((rows, CHUNK_WIDTH)),
        out_shape=jax.ShapeDtypeStruct((g, rows, CHUNK_WIDTH), F32),
        scratch_shapes=[pltpu.VMEM((scratch_rows, STATE_WIDTH), F32)] * 6,
        compiler_params=_cparams(("parallel",)),
        name="s5_scan",
    )(uc, w1, w2, coef, d_rows)


def _mixout_kernel(x_ref, a_ref, yc_ref, gw_ref, gb_ref, ga_ref, gs_ref, wo_ref, o_ref, y_tok, *, aw):
    _chunks_to_tokens(yc_ref, y_tok, x_ref.shape[0] // SSM_CHUNK)
    y = jnp.concatenate([y_tok[blk] for blk in range(y_tok.shape[0])], axis=1)
    yg = 0.5 * y * (1.0 + jnp.tanh(math.sqrt(2.0 / math.pi) * (y + 0.044715 * (y * y * y))))
    gate = _sigmoid(jnp.dot(yg.astype(BF16), gw_ref[...], preferred_element_type=F32) + gb_ref[...])
    an = _rms(a_ref[...], ga_ref[...]).astype(BF16)
    sn = _rms(yg * gate, gs_ref[...]).astype(BF16)
    o_ref[...] = (x_ref[...] + jnp.dot(an, wo_ref[0:aw, :], preferred_element_type=F32)
                  + jnp.dot(sn, wo_ref[aw:, :], preferred_element_type=F32))


def _mixout(x, attn, yc, glu_w, glu_b, g_attn, g_ssm, w_out):
    t, d = x.shape
    aw = attn.shape[1]
    groups = yc.shape[0]
    sw = groups * SSM_GROUP
    tm = TOKEN_TILE
    row = lambda i: (i, 0)
    fixed = lambda i: (0, 0)
    return pl.pallas_call(
        functools.partial(_mixout_kernel, aw=aw),
        grid=(t // tm,),
        in_specs=[pl.BlockSpec((tm, d), row), pl.BlockSpec((tm, aw), row),
                  pl.BlockSpec((groups, tm // SSM_CHUNK, CHUNK_WIDTH), lambda i: (0, i, 0)),
                  pl.BlockSpec((sw, sw), fixed), pl.BlockSpec((1, sw), fixed), pl.BlockSpec((1, aw), fixed),
                  pl.BlockSpec((1, sw), fixed), pl.BlockSpec((d, d), fixed)],
        out_specs=pl.BlockSpec((tm, d), row),
        out_shape=jax.ShapeDtypeStruct((t, d), F32),
        scratch_shapes=[pltpu.VMEM((sw // LANES, tm, LANES), F32)],
        compiler_params=_cparams(("parallel",)),
        name="mixer_out",
    )(x, attn, yc, glu_w, glu_b.reshape(1, sw), g_attn.reshape(1, aw), g_ssm.reshape(1, sw), w_out)


MXU_WIDTH = 256
FF_CHUNK_TILES = 6


def _ff_chunks(d_ff):
    assert d_ff % MXU_WIDTH == 0
    tiles = d_ff // MXU_WIDTH
    n = -(-tiles // FF_CHUNK_TILES)
    sizes = [(tiles // n + (1 if c < tiles % n else 0)) * MXU_WIDTH for c in range(n)]
    return [(sum(sizes[:c]), sizes[c]) for c in range(n)]


def _swiglu(h, w1_ref, w3_ref, w2_ref):
    d_ff = w1_ref.shape[1]
    acc = None
    for start, size in _ff_chunks(d_ff):
        a = jnp.dot(h, w1_ref[:, start:start + size], preferred_element_type=F32)
        b = jnp.dot(h, w3_ref[:, start:start + size], preferred_element_type=F32)
        act = (a * _sigmoid(a) * b).astype(BF16)
        part = jnp.dot(act, w2_ref[start:start + size, :], preferred_element_type=F32)
        acc = part if acc is None else acc + part
    return acc


def _ffn_kernel(x_ref, g_ref, w1_ref, w3_ref, w2_ref, o_ref):
    x = x_ref[...]
    h = _rms(x, g_ref[...]).astype(BF16)
    o_ref[...] = x + _swiglu(h, w1_ref, w3_ref, w2_ref)


def _dense_ffn(x, g, w1, w3, w2):
    t, d = x.shape
    d_ff = w1.shape[1]
    tm = TOKEN_TILE
    row = lambda i: (i, 0)
    fixed = lambda i: (0, 0)
    resident = lambda shape: pl.BlockSpec(shape, fixed, pipeline_mode=pl.Buffered(1))
    return pl.pallas_call(
        _ffn_kernel,
        grid=(t // tm,),
        in_specs=[pl.BlockSpec((tm, d), row), pl.BlockSpec((1, d), fixed),
                  resident((d, d_ff)), resident((d, d_ff)), resident((d_ff, d))],
        out_specs=pl.BlockSpec((tm, d), row),
        out_shape=jax.ShapeDtypeStruct((t, d), F32),
        compiler_params=_cparams(("parallel",)),
        name="dense_ffn",
    )(x, g.reshape(1, d), w1, w3, w2)


def _router_kernel(x_ref, g_ref, r_ref, idx_ref, wt_ref, *, n_experts):
    h = _rms(x_ref[...], g_ref[...])
    logits = jnp.dot(h, r_ref[...], preferred_element_type=F32, precision=lax.Precision.HIGHEST)
    lane = lax.broadcasted_iota(jnp.int32, logits.shape, 1).astype(F32)
    logits = jnp.where(lane < n_experts, logits, -jnp.inf)
    v1 = jnp.max(logits, axis=1, keepdims=True)
    i1 = jnp.min(jnp.where(logits == v1, lane, float(LANES)), axis=1, keepdims=True)
    rest = jnp.where(lane == i1, -jnp.inf, logits)
    v2 = jnp.max(rest, axis=1, keepdims=True)
    i2 = jnp.min(jnp.where(rest == v2, lane, float(LANES)), axis=1, keepdims=True)
    e = jnp.exp(v2 - v1)
    w1 = 1.0 / (1.0 + e)
    w2 = e / (1.0 + e)
    idx_ref[...] = jnp.where(lane == 0.0, i1, jnp.where(lane == 1.0, i2, 0.0))[:, :8].astype(jnp.int32)
    wt_ref[...] = jnp.where(lane == 0.0, w1, jnp.where(lane == 1.0, w2, 0.0))[:, :8]


def _router(x, g, router):
    t, d = x.shape
    n_experts = router.shape[1]
    tm = TOKEN_TILE
    r_pad = jnp.zeros((d, LANES), F32).at[:, :n_experts].set(router.astype(F32))
    row = lambda i: (i, 0)
    fixed = lambda i: (0, 0)
    return pl.pallas_call(
        functools.partial(_router_kernel, n_experts=n_experts),
        grid=(t // tm,),
        in_specs=[pl.BlockSpec((tm, d), row), pl.BlockSpec((1, d), fixed), pl.BlockSpec((d, LANES), fixed)],
        out_specs=[pl.BlockSpec((tm, 8), row), pl.BlockSpec((tm, 8), row)],
        out_shape=[jax.ShapeDtypeStruct((t, 8), jnp.int32), jax.ShapeDtypeStruct((t, 8), F32)],
        compiler_params=_cparams(("parallel",)),
        name="router",
    )(x, g.reshape(1, d), r_pad)


def _row_copy(src, src_row, dst, dst_row, sem):
    return pltpu.make_async_copy(src.at[pl.ds(src_row, 1), :], dst.at[pl.ds(dst_row, 1), :], sem)


def _moe_kernel(te_ref, src0_ref, srcn_ref, dst_ref, x_hbm, g_ref, w1_ref, w3_ref, w2_ref, o_hbm,
                xbuf, ybuf, gsem, ssem, *, tm):
    del te_ref
    i = pl.program_id(0)
    n_tiles = pl.num_programs(0)
    slot = lax.rem(i, 2)
    other = 1 - slot

    def start_gather(idx_ref, s):
        for j in range(tm):
            _row_copy(x_hbm, idx_ref[0, j], xbuf.at[s], j, gsem.at[s]).start()

    def wait_gather(s):
        for j in range(tm):
            _row_copy(x_hbm, 0, xbuf.at[s], j, gsem.at[s]).wait()

    def wait_scatter(s):
        for j in range(tm):
            _row_copy(ybuf.at[s], j, o_hbm, 0, ssem.at[s]).wait()

    @pl.when(i == 0)
    def _():
        start_gather(src0_ref, 0)

    @pl.when(i >= 2)
    def _():
        wait_scatter(slot)

    wait_gather(slot)
    start_gather(srcn_ref, other)
    h = _rms(xbuf[slot], g_ref[...]).astype(BF16)
    ybuf[slot] = _swiglu(h, w1_ref, w3_ref, w2_ref)
    for j in range(tm):
        _row_copy(ybuf.at[slot], j, o_hbm, dst_ref[0, j], ssem.at[slot]).start()

    @pl.when(i == n_tiles - 1)
    def _():
        wait_gather(other)
        wait_scatter(slot)

        @pl.when(n_tiles >= 2)
        def _():
            wait_scatter(other)


def _moe_plan(idx, n_experts, tm):
    i32 = jnp.int32
    t = idx.shape[0]
    n_assign = TOP_K * t
    pad = n_experts * tm
    n_rows = n_assign + pad
    n_tiles = n_rows // tm
    flat = idx.reshape(-1)
    experts = jnp.arange(n_experts, dtype=i32)
    keys = jnp.sort(flat * n_assign + jnp.arange(n_assign, dtype=i32))
    counts = jnp.sum((flat[:, None] == experts[None, :]).astype(i32), axis=0)
    padded = (counts + tm - 1) // tm * tm
    ends = jnp.cumsum(padded)
    off = jnp.cumsum(counts) - counts
    poff = ends - padded
    rows = jnp.arange(n_rows, dtype=i32)
    e_row = jnp.minimum(jnp.sum((rows[:, None] >= ends[None, :]).astype(i32), axis=1), n_experts - 1)
    onehot = (e_row[:, None] == experts[None, :]).astype(i32)
    rel = rows - jnp.sum(onehot * poff[None, :], axis=1)
    count_row = jnp.sum(onehot * counts[None, :], axis=1)
    valid = rel < count_row
    keys_padded = jnp.concatenate([keys, jnp.zeros((pad,), i32)])
    key_row = jnp.zeros((n_rows,), i32)
    for e in range(n_experts):
        key_row = jnp.where(e_row == e, jnp.roll(keys_padded, poff[e] - off[e]), key_row)
    assign = key_row - e_row * n_assign
    token, slot = assign // TOP_K, assign % TOP_K
    valid_before = jnp.sum(onehot * off[None, :], axis=1) + jnp.minimum(rel, count_row)
    src = jnp.where(valid, token, 0)
    dst = jnp.where(valid, slot * t + token, n_assign + rows - valid_before)
    tile_e = e_row.reshape(n_tiles, tm)[:, 0]
    return src.reshape(n_tiles, 1, tm), dst.reshape(n_tiles, 1, tm), tile_e


def _moe_experts(x, g, w1, w3, w2, src, dst, tile_e):
    t, d = x.shape
    d_ff = w1.shape[2]
    n_tiles, _, tm = src.shape
    smem_row = lambda fn: pl.BlockSpec((None, 1, tm), fn, memory_space=pltpu.SMEM)
    expert = lambda shape: pl.BlockSpec((None,) + shape, lambda i, te: (te[i], 0, 0))
    grid_spec = pltpu.PrefetchScalarGridSpec(
        num_scalar_prefetch=1,
        grid=(n_tiles,),
        in_specs=[smem_row(lambda i, te: (0, 0, 0)),
                  smem_row(lambda i, te: (jnp.minimum(i + 1, n_tiles - 1), 0, 0)),
                  smem_row(lambda i, te: (i, 0, 0)),
                  pl.BlockSpec(memory_space=pl.ANY),
                  pl.BlockSpec((1, d), lambda i, te: (0, 0)),
                  expert((d, d_ff)), expert((d, d_ff)), expert((d_ff, d))],
        out_specs=pl.BlockSpec(memory_space=pl.ANY),
        scratch_shapes=[pltpu.VMEM((2, tm, d), F32), pltpu.VMEM((2, tm, d), F32),
                        pltpu.SemaphoreType.DMA((2,)), pltpu.SemaphoreType.DMA((2,))],
    )
    return pl.pallas_call(
        functools.partial(_moe_kernel, tm=tm),
        grid_spec=grid_spec,
        out_shape=jax.ShapeDtypeStruct((n_tiles * tm, d), F32),
        compiler_params=_cparams(("arbitrary",)),
        name="moe_experts",
    )(tile_e, src, src, dst, x, g.reshape(1, d), w1, w3, w2)


def _combine_kernel(x_ref, y0_ref, y1_ref, wt_ref, g_ref, o_ref, *, final_norm):
    wt = wt_ref[...]
    out = x_ref[...] + wt[:, 0:1] * y0_ref[...] + wt[:, 1:2] * y1_ref[...]
    if final_norm:
        out = _rms(out, g_ref[...])
    o_ref[...] = out


def _moe_combine(x, y2, wt, g_final, final_norm):
    t, d = x.shape
    tm = TOKEN_TILE
    row = lambda i: (i, 0)
    return pl.pallas_call(
        functools.partial(_combine_kernel, final_norm=final_norm),
        grid=(t // tm,),
        in_specs=[pl.BlockSpec((tm, d), row), pl.BlockSpec((tm, d), row),
                  pl.BlockSpec((tm, d), lambda i: (i + t // tm, 0)),
                  pl.BlockSpec((tm, 8), row), pl.BlockSpec((1, d), lambda i: (0, 0))],
        out_specs=pl.BlockSpec((tm, d), row),
        out_shape=jax.ShapeDtypeStruct((t, d), F32),
        compiler_params=_cparams(("parallel",)),
        name="moe_combine",
    )(x, y2, y2, wt, g_final.reshape(1, d))


def _norm_kernel(x_ref, g_ref, o_ref):
    o_ref[...] = _rms(x_ref[...], g_ref[...])


def _final_norm(x, g):
    t, d = x.shape
    tm = TOKEN_TILE
    return pl.pallas_call(
        _norm_kernel,
        grid=(t // tm,),
        in_specs=[pl.BlockSpec((tm, d), lambda i: (i, 0)), pl.BlockSpec((1, d), lambda i: (0, 0))],
        out_specs=pl.BlockSpec((tm, d), lambda i: (i, 0)),
        out_shape=jax.ShapeDtypeStruct((t, d), F32),
        compiler_params=_cparams(("parallel",)),
        name="final_norm",
    )(x, g.reshape(1, d))


def _trunk(x, p):
    b, s, d = x.shape
    t = b * s
    aw = p['attn_out_norm'].shape[1]
    depth = p['w_in'].shape[0]
    xt = x.reshape(t, d)
    normed = False
    for layer in range(depth):
        q, k, v, uc = _inproj(xt, p['norm_mix'][layer], p['w_in'][layer], aw)
        shape3 = lambda z: z.reshape(b, s, z.shape[1])
        attn = _attention(shape3(q), shape3(k), shape3(v), p['bias_tiles'][s]).reshape(t, aw)
        w1m, w2m, coef = p['ssm'][layer]
        yc = _ssm_scan(uc, w1m, w2m, coef, p['ssm_d'][layer], b)
        xt = _mixout(xt, attn, yc, p['glu_w'][layer], p['glu_b'][layer], p['attn_out_norm'][layer],
                     p['ssm_out_norm'][layer], p['w_out'][layer])
        g = p['norm_ffn'][layer]
        j = layer // 2
        if layer % 2 == 0:
            xt = _dense_ffn(xt, g, p['ffn_w1'][j], p['ffn_w3'][j], p['ffn_w2'][j])
        else:
            idx, wt = _router(xt, g, p['moe_router'][j])
            n_experts = p['moe_router'].shape[2]
            src, dst, tile_e = _moe_plan(idx[:, :TOP_K], n_experts, MOE_TILE)
            y2 = _moe_experts(xt, g, p['moe_w1'][j], p['moe_w3'][j], p['moe_w2'][j], src, dst, tile_e)
            last = layer == depth - 1
            xt = _moe_combine(xt, y2, wt, p['final_norm'], last)
            normed = last
    if not normed:
        xt = _final_norm(xt, p['final_norm'])
    return xt.reshape(b, s, d)


def kernel(x_prompt, x_sample, rel_bias, norm_mix, w_in, ssm_lambda_re, ssm_lambda_im, ssm_log_step, ssm_b_re, ssm_b_im, ssm_c_re, ssm_c_im, ssm_d, glu_w, glu_b, attn_out_norm, ssm_out_norm, w_out, norm_ffn, ffn_w1, ffn_w3, ffn_w2, moe_router, moe_w1, moe_w3, moe_w2, final_norm):
    depth = w_in.shape[0]
    bias_tiles = {}
    for x in (x_prompt, x_sample):
        s = x.shape[1]
        if s not in bias_tiles:
            bias_tiles[s] = [_bias_tiles(rel_bias, dil, s // dil) for dil in DILATIONS]
    p = {
        'bias_tiles': bias_tiles,
        'norm_mix': norm_mix, 'w_in': w_in.astype(BF16),
        'ssm': [_ssm_matrices(ssm_lambda_re[l], ssm_lambda_im[l], ssm_log_step[l], ssm_b_re[l], ssm_b_im[l],
                              ssm_c_re[l], ssm_c_im[l]) for l in range(depth)],
        'ssm_d': ssm_d, 'glu_w': glu_w.astype(BF16), 'glu_b': glu_b,
        'attn_out_norm': attn_out_norm, 'ssm_out_norm': ssm_out_norm, 'w_out': w_out.astype(BF16),
        'norm_ffn': norm_ffn,
        'ffn_w1': ffn_w1.astype(BF16), 'ffn_w3': ffn_w3.astype(BF16), 'ffn_w2': ffn_w2.astype(BF16),
        'moe_router': moe_router,
        'moe_w1': moe_w1.astype(BF16), 'moe_w3': moe_w3.astype(BF16), 'moe_w2': moe_w2.astype(BF16),
        'final_norm': final_norm,
    }
    return (_trunk(x_prompt, p), _trunk(x_sample, p))
```

```python
import functools
import math

import numpy as np
import jax
import jax.numpy as jnp
from jax import lax
from jax.experimental import pallas as pl
from jax.experimental.pallas import tpu as pltpu

F32 = jnp.float32
BF16 = jnp.bfloat16

LANES = 128
VMEM_LIMIT_BYTES = 56 * 1024 * 1024

HEAD_DIM = 64
HEADS_PER_BLOCK = LANES // HEAD_DIM
DILATIONS = (1, 4, 16)
BAND_HALF = 64
Q_BLOCK = 128
N_REL_BUCKETS = 32
REL_MAX_DISTANCE = 1024
SSM_GROUP = 16
SSM_STATE = 64
GROUPS_PER_BLOCK = LANES // SSM_GROUP
SSM_CHUNK = 16
CHUNK_WIDTH = SSM_CHUNK * SSM_GROUP
STATE_WIDTH = 2 * SSM_STATE
TOP_K = 2
NORM_EPS = 1e-6
MASK_VALUE = -1e30

TOKEN_TILE = 512
MOE_TILE = 256


def _cparams(semantics):
    return pltpu.CompilerParams(dimension_semantics=semantics, vmem_limit_bytes=VMEM_LIMIT_BYTES)


def _rms(x, g):
    return x * lax.rsqrt(jnp.mean(x * x, axis=-1, keepdims=True) + NORM_EPS) * g


def _sigmoid(x):
    return 1.0 / (1.0 + jnp.exp(-x))


def _granule_transpose(vs):
    lane_granule = lax.broadcasted_iota(jnp.int32, vs[0].shape, 1) // SSM_GROUP
    for dist in (4, 2, 1):
        upper = (lane_granule & dist) != 0
        new = list(vs)
        for i in range(GROUPS_PER_BLOCK):
            if i & dist == 0:
                j = i + dist
                new[i] = jnp.where(upper, pltpu.roll(vs[j], SSM_GROUP * dist, 1), vs[i])
                new[j] = jnp.where(upper, vs[j], pltpu.roll(vs[i], LANES - SSM_GROUP * dist, 1))
        vs = new
    return vs


def _tokens_to_chunks(tok_ref, chunk_ref, n_chunks):
    for blk in range(tok_ref.shape[0]):
        for half in range(CHUNK_WIDTH // LANES):
            by_token = [tok_ref[blk, pl.ds(half * GROUPS_PER_BLOCK + t, n_chunks, stride=SSM_CHUNK), :]
                        for t in range(GROUPS_PER_BLOCK)]
            for g, val in enumerate(_granule_transpose(by_token)):
                chunk_ref[blk * GROUPS_PER_BLOCK + g, :, half * LANES:(half + 1) * LANES] = val


def _chunks_to_tokens(chunk_ref, tok_ref, n_chunks):
    for blk in range(tok_ref.shape[0]):
        for half in range(CHUNK_WIDTH // LANES):
            by_group = [chunk_ref[blk * GROUPS_PER_BLOCK + g, :, half * LANES:(half + 1) * LANES]
                        for g in range(GROUPS_PER_BLOCK)]
            for t, val in enumerate(_granule_transpose(by_group)):
                tok_ref[blk, pl.ds(half * GROUPS_PER_BLOCK + t, n_chunks, stride=SSM_CHUNK), :] = val


def _inproj_kernel(x_ref, g_ref, w_ref, q_ref, k_ref, v_ref, uc_ref, u_tok, *, aw):
    h = _rms(x_ref[...], g_ref[...]).astype(BF16)
    scale = HEAD_DIM ** -0.5
    q_ref[...] = (jnp.dot(h, w_ref[:, 0:aw], preferred_element_type=F32) * scale).astype(BF16)
    k_ref[...] = jnp.dot(h, w_ref[:, aw:2 * aw], preferred_element_type=F32).astype(BF16)
    v_ref[...] = jnp.dot(h, w_ref[:, 2 * aw:3 * aw], preferred_element_type=F32).astype(BF16)
    u = jnp.dot(h, w_ref[:, 3 * aw:], preferred_element_type=F32)
    for blk in range(u_tok.shape[0]):
        u_tok[blk] = u[:, blk * LANES:(blk + 1) * LANES]
    _tokens_to_chunks(u_tok, uc_ref, x_ref.shape[0] // SSM_CHUNK)


def _inproj(x, g, w_bf16, aw):
    t, d = x.shape
    n = w_bf16.shape[1]
    sw = n - 3 * aw
    tm = TOKEN_TILE
    groups, tile_chunks = sw // SSM_GROUP, tm // SSM_CHUNK
    row = lambda i: (i, 0)
    fixed = lambda i: (0, 0)
    return pl.pallas_call(
        functools.partial(_inproj_kernel, aw=aw),
        grid=(t // tm,),
        in_specs=[pl.BlockSpec((tm, d), row), pl.BlockSpec((1, d), fixed), pl.BlockSpec((d, n), fixed)],
        out_specs=[pl.BlockSpec((tm, aw), row), pl.BlockSpec((tm, aw), row), pl.BlockSpec((tm, aw), row),
                   pl.BlockSpec((groups, tile_chunks, CHUNK_WIDTH), lambda i: (0, i, 0))],
        out_shape=[jax.ShapeDtypeStruct((t, aw), BF16)] * 3
        + [jax.ShapeDtypeStruct((groups, t // SSM_CHUNK, CHUNK_WIDTH), F32)],
        scratch_shapes=[pltpu.VMEM((sw // LANES, tm, LANES), F32)],
        compiler_params=_cparams(("parallel",)),
        name="inproj",
    )(x, g.reshape(1, d), w_bf16)


def _t5_bucket(rel):
    nb = N_REL_BUCKETS // 2
    max_exact = nb // 2
    ret = (rel > 0).astype(np.int32) * nb
    n = np.abs(rel)
    large = max_exact + (np.log(np.maximum(n, 1) / max_exact)
                         / np.log(REL_MAX_DISTANCE / max_exact) * (nb - max_exact)).astype(np.int32)
    large = np.minimum(large, nb - 1)
    return (ret + np.where(n < max_exact, n, large)).astype(np.int32)


def _key_window(sub_len):
    return min(2 * Q_BLOCK, sub_len)


def _bias_tiles(rel_bias, dilation, sub_len):
    w = _key_window(sub_len)
    offsets = (0, -BAND_HALF, -2 * BAND_HALF) if sub_len > Q_BLOCK else (0,)
    i = np.arange(Q_BLOCK)[:, None]
    j = np.arange(w)[None, :]
    tiles = []
    for off in offsets:
        rel = j + off - i
        bias = rel_bias[jnp.asarray(_t5_bucket(rel * dilation))].astype(F32)
        bias = jnp.where(jnp.asarray(np.abs(rel) <= BAND_HALF)[..., None], bias, MASK_VALUE)
        tiles.append(jnp.transpose(bias, (2, 0, 1)))
    return jnp.stack(tiles)


def _attn_tile(qb, kb, vb, bias0, bias1):
    lane = lax.broadcasted_iota(jnp.int32, qb.shape, 1)
    parts = []
    for head, bias in enumerate((bias0, bias1)):
        in_head = (lane < HEAD_DIM) if head == 0 else (lane >= HEAD_DIM)
        qh = jnp.where(in_head, qb, jnp.zeros_like(qb))
        s = lax.dot_general(qh, kb, (((1,), (1,)), ((), ())), preferred_element_type=F32) + bias
        m = jnp.max(s, axis=1, keepdims=True)
        p = jnp.exp(s - m)
        l = jnp.sum(p, axis=1, keepdims=True)
        o = jnp.dot(p.astype(BF16), vb, preferred_element_type=F32)
        parts.append((o, m, l))
    first = lax.broadcasted_iota(jnp.int32, (Q_BLOCK, LANES), 1) < HEAD_DIM
    o = jnp.where(first, parts[0][0], parts[1][0])
    m = jnp.where(first, parts[0][1], parts[1][1])
    l = jnp.where(first, parts[0][2], parts[1][2])
    return o, m, l


ATTN_UNROLL = 8
ATTN_STAGE_ROWS = 2048
COPY_ROWS = 512


def _attn_kernel(q_ref, k_ref, v_ref, b1_ref, b4_ref, b16_ref, o_ref,
                 stage, qd, kd, vd, m_acc, l_acc, *, seq):
    for dil, bias_ref in ((DILATIONS[2], b16_ref), (DILATIONS[1], b4_ref), (DILATIONS[0], b1_ref)):
        first_branch = dil == DILATIONS[2]
        sub_len = seq // dil
        n_blocks = sub_len // Q_BLOCK
        assert n_blocks & (n_blocks - 1) == 0 and (dil * n_blocks) % ATTN_UNROLL == 0
        w = _key_window(sub_len)

        if dil == 1:
            qs, ks, vs = q_ref, k_ref, v_ref
        else:
            seg = min(ATTN_STAGE_ROWS, seq)
            per = seg // dil

            def segment(sg, carry, dil=dil, sub_len=sub_len, seg=seg, per=per):
                base = pl.multiple_of(sg * seg, seg)
                for idx, src in enumerate((q_ref, k_ref, v_ref)):
                    for c in range(seg // COPY_ROWS):
                        stage[idx, c * COPY_ROWS:(c + 1) * COPY_ROWS, :] = (
                            src[pl.ds(base + c * COPY_ROWS, COPY_ROWS), :].astype(F32))

                def residue(r, inner):
                    dst = pl.ds(pl.multiple_of(r * sub_len + sg * per, Q_BLOCK), per)
                    for idx, dst_ref in enumerate((qd, kd, vd)):
                        dst_ref[dst, :] = stage[idx, pl.ds(r, per, stride=dil), :].astype(BF16)
                    return inner

                lax.fori_loop(0, dil, residue, 0)
                return carry

            lax.fori_loop(0, seq // seg, segment, 0)
            qs, ks, vs = qd, kd, vd

        def trip(j, carry, dil=dil, bias_ref=bias_ref, sub_len=sub_len, n_blocks=n_blocks, w=w,
                 qs=qs, ks=ks, vs=vs, first_branch=first_branch):
            for uu in range(ATTN_UNROLL):
                item = j * ATTN_UNROLL + uu
                if n_blocks == 1:
                    r, l0, ws, var = item, 0, 0, 0
                else:
                    r = jnp.right_shift(item, n_blocks.bit_length() - 1) if dil > 1 else 0
                    i = jnp.bitwise_and(item, n_blocks - 1) if dil > 1 else item
                    l0 = i * Q_BLOCK
                    ws = jnp.clip(l0 - BAND_HALF, 0, sub_len - w)
                    var = jnp.where(i == 0, 0, jnp.where(i == n_blocks - 1, 2, 1))
                base = r * sub_len
                qb = qs[pl.ds(pl.multiple_of(base + l0, Q_BLOCK), Q_BLOCK), :]
                kb = ks[pl.ds(pl.multiple_of(base + ws, BAND_HALF), w), :]
                vb = vs[pl.ds(pl.multiple_of(base + ws, BAND_HALF), w), :]
                o, m, l = _attn_tile(qb, kb, vb, bias_ref[var, 0], bias_ref[var, 1])
                if dil == 1:
                    rows = pl.ds(pl.multiple_of(l0, Q_BLOCK), Q_BLOCK)
                else:
                    rows = pl.ds(r + dil * l0, Q_BLOCK, stride=dil)
                if first_branch:
                    o_ref[rows, :] = o
                    m_acc[rows, :] = m
                    l_acc[rows, :] = l
                else:
                    m_old = m_acc[rows, :]
                    m_new = jnp.maximum(m_old, m)
                    a = jnp.exp(m_old - m_new)
                    b = jnp.exp(m - m_new)
                    o_ref[rows, :] = a * o_ref[rows, :] + b * o
                    l_acc[rows, :] = a * l_acc[rows, :] + b * l
                    m_acc[rows, :] = m_new
            return carry

        lax.fori_loop(0, dil * n_blocks // ATTN_UNROLL, trip, 0)

    def finish(c, carry):
        rows = pl.ds(pl.multiple_of(c * COPY_ROWS, COPY_ROWS), COPY_ROWS)
        o_ref[rows, :] = o_ref[rows, :] / l_acc[rows, :]
        return carry

    lax.fori_loop(0, seq // COPY_ROWS, finish, 0)


def _attention(q, k, v, bias_tiles):
    b, s, a = q.shape
    n_pairs = a // LANES
    assert s % (DILATIONS[-1] * Q_BLOCK) == 0
    seq_spec = pl.BlockSpec((None, s, LANES), lambda bi, hp: (bi, 0, hp))
    bias_specs = [pl.BlockSpec((t.shape[0], HEADS_PER_BLOCK) + t.shape[2:], lambda bi, hp: (0, hp, 0, 0))
                  for t in bias_tiles]
    return pl.pallas_call(
        functools.partial(_attn_kernel, seq=s),
        grid=(b, n_pairs),
        in_specs=[seq_spec, seq_spec, seq_spec] + bias_specs,
        out_specs=seq_spec,
        out_shape=jax.ShapeDtypeStruct((b, s, a), F32),
        scratch_shapes=[pltpu.VMEM((3, min(ATTN_STAGE_ROWS, s), LANES), F32)]
        + [pltpu.VMEM((s, LANES), BF16)] * 3
        + [pltpu.VMEM((s, LANES), F32)] * 2,
        compiler_params=_cparams(("parallel", "parallel")),
        name="dilated_attn",
    )(q, k, v, *bias_tiles)


def _ssm_matrices(lam_re, lam_im, log_step, b_re, b_im, c_re, c_im):
    hp = lax.Precision.HIGHEST
    t_len, n_ch = SSM_CHUNK, SSM_GROUP
    step = jnp.exp(log_step.astype(F32))[..., None]
    lr, li = lam_re.astype(F32), lam_im.astype(F32)
    zr, zi = lr * step, li * step
    tau = jnp.arange(t_len + 1, dtype=F32)[:, None, None, None]
    mag = jnp.exp(tau * zr)
    pr, pi = mag * jnp.cos(tau * zi), mag * jnp.sin(tau * zi)
    nr, ni = pr[1] - 1.0, pi[1]
    den = lr * lr + li * li
    fr, fi = (nr * lr + ni * li) / den, (ni * lr - nr * li) / den
    bbr = fr[..., None] * b_re - fi[..., None] * b_im
    bbi = fr[..., None] * b_im + fi[..., None] * b_re
    cpr = c_re[None] * pr[:, :, :, None, :] - c_im[None] * pi[:, :, :, None, :]
    cpi = c_re[None] * pi[:, :, :, None, :] + c_im[None] * pr[:, :, :, None, :]
    kern = (jnp.einsum('tdgcn,dgnk->tdgck', cpr, bbr, precision=hp)
            - jnp.einsum('tdgcn,dgnk->tdgck', cpi, bbi, precision=hp))
    g = lam_re.shape[1]

    def toeplitz(k_dir, causal):
        taps = k_dir[:t_len]
        zeros = jnp.zeros_like(taps)
        if causal:
            line = jnp.concatenate([zeros, taps], axis=0)
            rows = [line[t_len - s:2 * t_len - s] for s in range(t_len)]
        else:
            line = jnp.concatenate([taps[::-1], zeros], axis=0)
            rows = [line[t_len - 1 - s:2 * t_len - 1 - s] for s in range(t_len)]
        m = jnp.stack(rows)
        return jnp.transpose(m, (2, 0, 4, 1, 3)).reshape(g, CHUNK_WIDTH, CHUNK_WIDTH)

    m_sum = toeplitz(kern[:, 0], True) + toeplitz(kern[:, 1], False)

    def state_in(direction, p_r, p_i):
        p_r, p_i = p_r[:, direction], p_i[:, direction]
        re = p_r[..., None] * bbr[direction][None] - p_i[..., None] * bbi[direction][None]
        im = p_r[..., None] * bbi[direction][None] + p_i[..., None] * bbr[direction][None]
        re = jnp.transpose(re, (1, 0, 3, 2)).reshape(g, CHUNK_WIDTH, SSM_STATE)
        im = jnp.transpose(im, (1, 0, 3, 2)).reshape(g, CHUNK_WIDTH, SSM_STATE)
        return jnp.concatenate([re, im, im, re], axis=-1)

    w1 = jnp.concatenate([m_sum, state_in(0, pr[:t_len][::-1], pi[:t_len][::-1]),
                          state_in(1, pr[:t_len], pi[:t_len])], axis=-1)

    def state_out(direction, c_r, c_i):
        c_r, c_i = c_r[:, direction], c_i[:, direction]
        re = jnp.transpose(c_r, (1, 3, 0, 2)).reshape(g, SSM_STATE, CHUNK_WIDTH)
        im = jnp.transpose(-c_i, (1, 3, 0, 2)).reshape(g, SSM_STATE, CHUNK_WIDTH)
        return jnp.concatenate([re, im], axis=1)

    w2 = jnp.concatenate([state_out(0, cpr[1:], cpi[1:]),
                          state_out(1, cpr[1:][::-1], cpi[1:][::-1])], axis=1)
    ar, ai = pr[t_len], pi[t_len]
    coef = jnp.stack([jnp.concatenate([ar[0], ar[0]], -1), jnp.concatenate([-ai[0], ai[0]], -1),
                      jnp.concatenate([ar[1], ar[1]], -1), jnp.concatenate([-ai[1], ai[1]], -1)], axis=1)
    return w1.astype(BF16), w2.astype(BF16), coef


SUBLANES = 8
SCAN_PITCH_PAD = 8


def _ssm_kernel(u_ref, w1_ref, w2_ref, coef_ref, d_ref, y_ref, zf_loc, wf_loc, zb_loc, wb_loc,
                zf_in, zb_in, *, n_chunks, batch):
    rows_total = batch * n_chunks
    rc = min(512, rows_total)
    piece = min(rc, n_chunks)
    pitch = n_chunks + SCAN_PITCH_PAD
    sw = STATE_WIDTH
    octets = batch // SUBLANES

    def scratch_rows(c, p):
        first = c * rc + p * piece
        if piece == n_chunks:
            return pl.ds(pl.multiple_of((first // n_chunks) * pitch, SUBLANES), piece)
        seq = first // n_chunks
        return pl.ds(pl.multiple_of(seq * pitch + (first - seq * n_chunks), SUBLANES), piece)

    def local(c, carry):
        rows = pl.ds(pl.multiple_of(c * rc, rc), rc)
        u = u_ref[rows, :]
        res = jnp.dot(u.astype(BF16), w1_ref[...], preferred_element_type=F32)
        y_ref[rows, :] = res[:, :CHUNK_WIDTH] + u * d_ref[...]
        for p in range(rc // piece):
            srows = scratch_rows(c, p)
            for idx, dst in enumerate((zf_loc, wf_loc, zb_loc, wb_loc)):
                dst[srows, :] = res[p * piece:(p + 1) * piece,
                                    CHUNK_WIDTH + idx * sw:CHUNK_WIDTH + (idx + 1) * sw]
        return carry

    lax.fori_loop(0, rows_total // rc, local, 0)

    ar_f = jnp.broadcast_to(coef_ref[0:1, :], (SUBLANES, sw))
    ai_f = jnp.broadcast_to(coef_ref[1:2, :], (SUBLANES, sw))
    ar_b = jnp.broadcast_to(coef_ref[2:3, :], (SUBLANES, sw))
    ai_b = jnp.broadcast_to(coef_ref[3:4, :], (SUBLANES, sw))

    def step(k, state):
        new = []
        for oc in range(octets):
            zf, wf, zb, wb = state[4 * oc:4 * oc + 4]
            rf = pl.ds(oc * SUBLANES * pitch + k, SUBLANES, stride=pitch)
            rb = pl.ds(oc * SUBLANES * pitch + (n_chunks - 1 - k), SUBLANES, stride=pitch)
            zf_in[rf, :] = zf
            zb_in[rb, :] = zb
            new += [ar_f * zf + ai_f * wf + zf_loc[rf, :], ar_f * wf - ai_f * zf + wf_loc[rf, :],
                    ar_b * zb + ai_b * wb + zb_loc[rb, :], ar_b * wb - ai_b * zb + wb_loc[rb, :]]
        return tuple(new)

    zero = jnp.zeros((SUBLANES, sw), F32)
    lax.fori_loop(0, n_chunks, step, (zero,) * (4 * octets), unroll=4)

    def carried(c, carry):
        rows = pl.ds(pl.multiple_of(c * rc, rc), rc)
        pieces = [scratch_rows(c, p) for p in range(rc // piece)]
        zf = jnp.concatenate([zf_in[srows, :] for srows in pieces], axis=0)
        zb = jnp.concatenate([zb_in[srows, :] for srows in pieces], axis=0)
        y_ref[rows, :] += (jnp.dot(zf.astype(BF16), w2_ref[0:sw, :], preferred_element_type=F32)
                           + jnp.dot(zb.astype(BF16), w2_ref[sw:, :], preferred_element_type=F32))
        return carry

    lax.fori_loop(0, rows_total // rc, carried, 0)


def _ssm_scan(uc, w1, w2, coef, d_skip, batch):
    g, rows, _ = uc.shape
    n_chunks = rows // batch
    assert batch % SUBLANES == 0 and n_chunks % SUBLANES == 0
    d_rows = jnp.tile(d_skip.astype(F32).reshape(g, 1, SSM_GROUP), (1, SSM_CHUNK, 1)).reshape(g, 1, CHUNK_WIDTH)
    per_group = lambda shape: pl.BlockSpec((None,) + shape, lambda gi: (gi,) + (0,) * len(shape))
    scratch_rows = batch * (n_chunks + SCAN_PITCH_PAD)
    return pl.pallas_call(
        functools.partial(_ssm_kernel, n_chunks=n_chunks, batch=batch),
        grid=(g,),
        in_specs=[per_group((rows, CHUNK_WIDTH)), per_group(w1.shape[1:]), per_group(w2.shape[1:]),
                  per_group(coef.shape[1:]), per_group((1, CHUNK_WIDTH))],
        out_specs=per_group((rows, CHUNK_WIDTH)),
        out_shape=jax.ShapeDtypeStruct((g, rows, CHUNK_WIDTH), F32),
        scratch_shapes=[pltpu.VMEM((scratch_rows, STATE_WIDTH), F32)] * 6,
        compiler_params=_cparams(("parallel",)),
        name="s5_scan",
    )(uc, w1, w2, coef, d_rows)


def _mixout_kernel(x_ref, a_ref, yc_ref, gw_ref, gb_ref, ga_ref, gs_ref, wo_ref, o_ref, y_tok, *, aw):
    _chunks_to_tokens(yc_ref, y_tok, x_ref.shape[0] // SSM_CHUNK)
    y = jnp.concatenate([y_tok[blk] for blk in range(y_tok.shape[0])], axis=1)
    yg = 0.5 * y * (1.0 + jnp.tanh(math.sqrt(2.0 / math.pi) * (y + 0.044715 * (y * y * y))))
    gate = _sigmoid(jnp.dot(yg.astype(BF16), gw_ref[...], preferred_element_type=F32) + gb_ref[...])
    an = _rms(a_ref[...], ga_ref[...]).astype(BF16)
    sn = _rms(yg * gate, gs_ref[...]).astype(BF16)
    o_ref[...] = (x_ref[...] + jnp.dot(an, wo_ref[0:aw, :], preferred_element_type=F32)
                  + jnp.dot(sn, wo_ref[aw:, :], preferred_element_type=F32))


def _mixout(x, attn, yc, glu_w, glu_b, g_attn, g_ssm, w_out):
    t, d = x.shape
    aw = attn.shape[1]
    groups = yc.shape[0]
    sw = groups * SSM_GROUP
    tm = TOKEN_TILE
    row = lambda i: (i, 0)
    fixed = lambda i: (0, 0)
    return pl.pallas_call(
        functools.partial(_mixout_kernel, aw=aw),
        grid=(t // tm,),
        in_specs=[pl.BlockSpec((tm, d), row), pl.BlockSpec((tm, aw), row),
                  pl.BlockSpec((groups, tm // SSM_CHUNK, CHUNK_WIDTH), lambda i: (0, i, 0)),
                  pl.BlockSpec((sw, sw), fixed), pl.BlockSpec((1, sw), fixed), pl.BlockSpec((1, aw), fixed),
                  pl.BlockSpec((1, sw), fixed), pl.BlockSpec((d, d), fixed)],
        out_specs=pl.BlockSpec((tm, d), row),
        out_shape=jax.ShapeDtypeStruct((t, d), F32),
        scratch_shapes=[pltpu.VMEM((sw // LANES, tm, LANES), F32)],
        compiler_params=_cparams(("parallel",)),
        name="mixer_out",
    )(x, attn, yc, glu_w, glu_b.reshape(1, sw), g_attn.reshape(1, aw), g_ssm.reshape(1, sw), w_out)


MXU_WIDTH = 256
FF_CHUNK_TILES = 6


def _ff_chunks(d_ff):
    assert d_ff % MXU_WIDTH == 0
    tiles = d_ff // MXU_WIDTH
    n = -(-tiles // FF_CHUNK_TILES)
    sizes = [(tiles // n + (1 if c < tiles % n else 0)) * MXU_WIDTH for c in range(n)]
    return [(sum(sizes[:c]), sizes[c]) for c in range(n)]


def _swiglu(h, w1_ref, w3_ref, w2_ref):
    d_ff = w1_ref.shape[1]
    acc = None
    for start, size in _ff_chunks(d_ff):
        a = jnp.dot(h, w1_ref[:, start:start + size], preferred_element_type=F32)
        b = jnp.dot(h, w3_ref[:, start:start + size], preferred_element_type=F32)
        act = (a * _sigmoid(a) * b).astype(BF16)
        part = jnp.dot(act, w2_ref[start:start + size, :], preferred_element_type=F32)
        acc = part if acc is None else acc + part
    return acc


def _ffn_kernel(x_ref, g_ref, w1_ref, w3_ref, w2_ref, o_ref):
    x = x_ref[...]
    h = _rms(x, g_ref[...]).astype(BF16)
    o_ref[...] = x + _swiglu(h, w1_ref, w3_ref, w2_ref)


def _dense_ffn(x, g, w1, w3, w2):
    t, d = x.shape
    d_ff = w1.shape[1]
    tm = TOKEN_TILE
    row = lambda i: (i, 0)
    fixed = lambda i: (0, 0)
    resident = lambda shape: pl.BlockSpec(shape, fixed, pipeline_mode=pl.Buffered(1))
    return pl.pallas_call(
        _ffn_kernel,
        grid=(t // tm,),
        in_specs=[pl.BlockSpec((tm, d), row), pl.BlockSpec((1, d), fixed),
                  resident((d, d_ff)), resident((d, d_ff)), resident((d_ff, d))],
        out_specs=pl.BlockSpec((tm, d), row),
        out_shape=jax.ShapeDtypeStruct((t, d), F32),
        compiler_params=_cparams(("parallel",)),
        name="dense_ffn",
    )(x, g.reshape(1, d), w1, w3, w2)


def _router_kernel(x_ref, g_ref, r_ref, idx_ref, wt_ref, *, n_experts):
    h = _rms(x_ref[...], g_ref[...])
    logits = jnp.dot(h, r_ref[...], preferred_element_type=F32, precision=lax.Precision.HIGHEST)
    lane = lax.broadcasted_iota(jnp.int32, logits.shape, 1).astype(F32)
    logits = jnp.where(lane < n_experts, logits, -jnp.inf)
    v1 = jnp.max(logits, axis=1, keepdims=True)
    i1 = jnp.min(jnp.where(logits == v1, lane, float(LANES)), axis=1, keepdims=True)
    rest = jnp.where(lane == i1, -jnp.inf, logits)
    v2 = jnp.max(rest, axis=1, keepdims=True)
    i2 = jnp.min(jnp.where(rest == v2, lane, float(LANES)), axis=1, keepdims=True)
    e = jnp.exp(v2 - v1)
    w1 = 1.0 / (1.0 + e)
    w2 = e / (1.0 + e)
    idx_ref[...] = jnp.where(lane == 0.0, i1, jnp.where(lane == 1.0, i2, 0.0))[:, :8].astype(jnp.int32)
    wt_ref[...] = jnp.where(lane == 0.0, w1, jnp.where(lane == 1.0, w2, 0.0))[:, :8]


def _router(x, g, router):
    t, d = x.shape
    n_experts = router.shape[1]
    tm = TOKEN_TILE
    r_pad = jnp.zeros((d, LANES), F32).at[:, :n_experts].set(router.astype(F32))
    row = lambda i: (i, 0)
    fixed = lambda i: (0, 0)
    return pl.pallas_call(
        functools.partial(_router_kernel, n_experts=n_experts),
        grid=(t // tm,),
        in_specs=[pl.BlockSpec((tm, d), row), pl.BlockSpec((1, d), fixed), pl.BlockSpec((d, LANES), fixed)],
        out_specs=[pl.BlockSpec((tm, 8), row), pl.BlockSpec((tm, 8), row)],
        out_shape=[jax.ShapeDtypeStruct((t, 8), jnp.int32), jax.ShapeDtypeStruct((t, 8), F32)],
        compiler_params=_cparams(("parallel",)),
        name="router",
    )(x, g.reshape(1, d), r_pad)


def _row_copy(src, src_row, dst, dst_row, sem):
    return pltpu.make_async_copy(src.at[pl.ds(src_row, 1), :], dst.at[pl.ds(dst_row, 1), :], sem)


def _moe_kernel(te_ref, src0_ref, srcn_ref, dst_ref, x_hbm, g_ref, w1_ref, w3_ref, w2_ref, o_hbm,
                xbuf, ybuf, gsem, ssem, *, tm):
    del te_ref
    i = pl.program_id(0)
    n_tiles = pl.num_programs(0)
    slot = lax.rem(i, 2)
    other = 1 - slot

    def start_gather(idx_ref, s):
        for j in range(tm):
            _row_copy(x_hbm, idx_ref[0, j], xbuf.at[s], j, gsem.at[s]).start()

    def wait_gather(s):
        for j in range(tm):
            _row_copy(x_hbm, 0, xbuf.at[s], j, gsem.at[s]).wait()

    def wait_scatter(s):
        for j in range(tm):
            _row_copy(ybuf.at[s], j, o_hbm, 0, ssem.at[s]).wait()

    @pl.when(i == 0)
    def _():
        start_gather(src0_ref, 0)

    @pl.when(i >= 2)
    def _():
        wait_scatter(slot)

    wait_gather(slot)
    start_gather(srcn_ref, other)
    h = _rms(xbuf[slot], g_ref[...]).astype(BF16)
    ybuf[slot] = _swiglu(h, w1_ref, w3_ref, w2_ref)
    for j in range(tm):
        _row_copy(ybuf.at[slot], j, o_hbm, dst_ref[0, j], ssem.at[slot]).start()

    @pl.when(i == n_tiles - 1)
    def _():
        wait_gather(other)
        wait_scatter(slot)

        @pl.when(n_tiles >= 2)
        def _():
            wait_scatter(other)


def _moe_plan(idx, n_experts, tm):
    i32 = jnp.int32
    t = idx.shape[0]
    n_assign = TOP_K * t
    pad = n_experts * tm
    n_rows = n_assign + pad
    n_tiles = n_rows // tm
    flat = idx.reshape(-1)
    experts = jnp.arange(n_experts, dtype=i32)
    keys = jnp.sort(flat * n_assign + jnp.arange(n_assign, dtype=i32))
    counts = jnp.sum((flat[:, None] == experts[None, :]).astype(i32), axis=0)
    padded = (counts + tm - 1) // tm * tm
    ends = jnp.cumsum(padded)
    off = jnp.cumsum(counts) - counts
    poff = ends - padded
    rows = jnp.arange(n_rows, dtype=i32)
    e_row = jnp.minimum(jnp.sum((rows[:, None] >= ends[None, :]).astype(i32), axis=1), n_experts - 1)
    onehot = (e_row[:, None] == experts[None, :]).astype(i32)
    rel = rows - jnp.sum(onehot * poff[None, :], axis=1)
    count_row = jnp.sum(onehot * counts[None, :], axis=1)
    valid = rel < count_row
    keys_padded = jnp.concatenate([keys, jnp.zeros((pad,), i32)])
    key_row = jnp.zeros((n_rows,), i32)
    for e in range(n_experts):
        key_row = jnp.where(e_row == e, jnp.roll(keys_padded, poff[e] - off[e]), key_row)
    assign = key_row - e_row * n_assign
    token, slot = assign // TOP_K, assign % TOP_K
    valid_before = jnp.sum(onehot * off[None, :], axis=1) + jnp.minimum(rel, count_row)
    src = jnp.where(valid, token, 0)
    dst = jnp.where(valid, slot * t + token, n_assign + rows - valid_before)
    tile_e = e_row.reshape(n_tiles, tm)[:, 0]
    return src.reshape(n_tiles, 1, tm), dst.reshape(n_tiles, 1, tm), tile_e


def _moe_experts(x, g, w1, w3, w2, src, dst, tile_e):
    t, d = x.shape
    d_ff = w1.shape[2]
    n_tiles, _, tm = src.shape
    smem_row = lambda fn: pl.BlockSpec((None, 1, tm), fn, memory_space=pltpu.SMEM)
    expert = lambda shape: pl.BlockSpec((None,) + shape, lambda i, te: (te[i], 0, 0))
    grid_spec = pltpu.PrefetchScalarGridSpec(
        num_scalar_prefetch=1,
        grid=(n_tiles,),
        in_specs=[smem_row(lambda i, te: (0, 0, 0)),
                  smem_row(lambda i, te: (jnp.minimum(i + 1, n_tiles - 1), 0, 0)),
                  smem_row(lambda i, te: (i, 0, 0)),
                  pl.BlockSpec(memory_space=pl.ANY),
                  pl.BlockSpec((1, d), lambda i, te: (0, 0)),
                  expert((d, d_ff)), expert((d, d_ff)), expert((d_ff, d))],
        out_specs=pl.BlockSpec(memory_space=pl.ANY),
        scratch_shapes=[pltpu.VMEM((2, tm, d), F32), pltpu.VMEM((2, tm, d), F32),
                        pltpu.SemaphoreType.DMA((2,)), pltpu.SemaphoreType.DMA((2,))],
    )
    return pl.pallas_call(
        functools.partial(_moe_kernel, tm=tm),
        grid_spec=grid_spec,
        out_shape=jax.ShapeDtypeStruct((n_tiles * tm, d), F32),
        compiler_params=_cparams(("arbitrary",)),
        name="moe_experts",
    )(tile_e, src, src, dst, x, g.reshape(1, d), w1, w3, w2)


def _combine_kernel(x_ref, y0_ref, y1_ref, wt_ref, g_ref, o_ref, *, final_norm):
    wt = wt_ref[...]
    out = x_ref[...] + wt[:, 0:1] * y0_ref[...] + wt[:, 1:2] * y1_ref[...]
    if final_norm:
        out = _rms(out, g_ref[...])
    o_ref[...] = out


def _moe_combine(x, y2, wt, g_final, final_norm):
    t, d = x.shape
    tm = TOKEN_TILE
    row = lambda i: (i, 0)
    return pl.pallas_call(
        functools.partial(_combine_kernel, final_norm=final_norm),
        grid=(t // tm,),
        in_specs=[pl.BlockSpec((tm, d), row), pl.BlockSpec((tm, d), row),
                  pl.BlockSpec((tm, d), lambda i: (i + t // tm, 0)),
                  pl.BlockSpec((tm, 8), row), pl.BlockSpec((1, d), lambda i: (0, 0))],
        out_specs=pl.BlockSpec((tm, d), row),
        out_shape=jax.ShapeDtypeStruct((t, d), F32),
        compiler_params=_cparams(("parallel",)),
        name="moe_combine",
    )(x, y2, y2, wt, g_final.reshape(1, d))


def _norm_kernel(x_ref, g_ref, o_ref):
    o_ref[...] = _rms(x_ref[...], g_ref[...])


def _final_norm(x, g):
    t, d = x.shape
    tm = TOKEN_TILE
    return pl.pallas_call(
        _norm_kernel,
        grid=(t // tm,),
        in_specs=[pl.BlockSpec((tm, d), lambda i: (i, 0)), pl.BlockSpec((1, d), lambda i: (0, 0))],
        out_specs=pl.BlockSpec((tm, d), lambda i: (i, 0)),
        out_shape=jax.ShapeDtypeStruct((t, d), F32),
        compiler_params=_cparams(("parallel",)),
        name="final_norm",
    )(x, g.reshape(1, d))


def _trunk(x, p):
    b, s, d = x.shape
    t = b * s
    aw = p['attn_out_norm'].shape[1]
    depth = p['w_in'].shape[0]
    xt = x.reshape(t, d)
    normed = False
    for layer in range(depth):
        q, k, v, uc = _inproj(xt, p['norm_mix'][layer], p['w_in'][layer], aw)
        shape3 = lambda z: z.reshape(b, s, z.shape[1])
        attn = _attention(shape3(q), shape3(k), shape3(v), p['bias_tiles'][s]).reshape(t, aw)
        w1m, w2m, coef = p['ssm'][layer]
        yc = _ssm_scan(uc, w1m, w2m, coef, p['ssm_d'][layer], b)
        xt = _mixout(xt, attn, yc, p['glu_w'][layer], p['glu_b'][layer], p['attn_out_norm'][layer],
                     p['ssm_out_norm'][layer], p['w_out'][layer])
        g = p['norm_ffn'][layer]
        j = layer // 2
        if layer % 2 == 0:
            xt = _dense_ffn(xt, g, p['ffn_w1'][j], p['ffn_w3'][j], p['ffn_w2'][j])
        else:
            idx, wt = _router(xt, g, p['moe_router'][j])
            n_experts = p['moe_router'].shape[2]
            src, dst, tile_e = _moe_plan(idx[:, :TOP_K], n_experts, MOE_TILE)
            y2 = _moe_experts(xt, g, p['moe_w1'][j], p['moe_w3'][j], p['moe_w2'][j], src, dst, tile_e)
            last = layer == depth - 1
            xt = _moe_combine(xt, y2, wt, p['final_norm'], last)
            normed = last
    if not normed:
        xt = _final_norm(xt, p['final_norm'])
    return xt.reshape(b, s, d)


def kernel(x_prompt, x_sample, rel_bias, norm_mix, w_in, ssm_lambda_re, ssm_lambda_im, ssm_log_step, ssm_b_re, ssm_b_im, ssm_c_re, ssm_c_im, ssm_d, glu_w, glu_b, attn_out_norm, ssm_out_norm, w_out, norm_ffn, ffn_w1, ffn_w3, ffn_w2, moe_router, moe_w1, moe_w3, moe_w2, final_norm):
    depth = w_in.shape[0]
    bias_tiles = {}
    for x in (x_prompt, x_sample):
        s = x.shape[1]
        if s not in bias_tiles:
            bias_tiles[s] = [_bias_tiles(rel_bias, dil, s // dil) for dil in DILATIONS]
    p = {
        'bias_tiles': bias_tiles,
        'norm_mix': norm_mix, 'w_in': w_in.astype(BF16),
        'ssm': [_ssm_matrices(ssm_lambda_re[l], ssm_lambda_im[l], ssm_log_step[l], ssm_b_re[l], ssm_b_im[l],
                              ssm_c_re[l], ssm_c_im[l]) for l in range(depth)],
        'ssm_d': ssm_d, 'glu_w': glu_w.astype(BF16), 'glu_b': glu_b,
        'attn_out_norm': attn_out_norm, 'ssm_out_norm': ssm_out_norm, 'w_out': w_out.astype(BF16),
        'norm_ffn': norm_ffn,
        'ffn_w1': ffn_w1.astype(BF16), 'ffn_w3': ffn_w3.astype(BF16), 'ffn_w2': ffn_w2.astype(BF16),
        'moe_router': moe_router,
        'moe_w1': moe_w1.astype(BF16), 'moe_w3': moe_w3.astype(BF16), 'moe_w2': moe_w2.astype(BF16),
        'final_norm': final_norm,
    }
    return (_trunk(x_prompt, p), _trunk(x_sample, p))
```

```python
import functools
import math

import numpy as np
import jax
import jax.numpy as jnp
from jax import lax
from jax.experimental import pallas as pl
from jax.experimental.pallas import tpu as pltpu

F32 = jnp.float32
BF16 = jnp.bfloat16

LANES = 128
VMEM_LIMIT_BYTES = 56 * 1024 * 1024

HEAD_DIM = 64
HEADS_PER_BLOCK = LANES // HEAD_DIM
DILATIONS = (1, 4, 16)
BAND_HALF = 64
Q_BLOCK = 128
N_REL_BUCKETS = 32
REL_MAX_DISTANCE = 1024
SSM_GROUP = 16
SSM_STATE = 64
GROUPS_PER_BLOCK = LANES // SSM_GROUP
SSM_CHUNK = 16
CHUNK_WIDTH = SSM_CHUNK * SSM_GROUP
STATE_WIDTH = 2 * SSM_STATE
TOP_K = 2
NORM_EPS = 1e-6
MASK_VALUE = -1e30

TOKEN_TILE = 512
MOE_TILE = 256


def _cparams(semantics):
    return pltpu.CompilerParams(dimension_semantics=semantics, vmem_limit_bytes=VMEM_LIMIT_BYTES)


def _rms(x, g):
    return x * lax.rsqrt(jnp.mean(x * x, axis=-1, keepdims=True) + NORM_EPS) * g


def _sigmoid(x):
    return 1.0 / (1.0 + jnp.exp(-x))


def _granule_transpose(vs):
    lane_granule = lax.broadcasted_iota(jnp.int32, vs[0].shape, 1) // SSM_GROUP
    for dist in (4, 2, 1):
        upper = (lane_granule & dist) != 0
        new = list(vs)
        for i in range(GROUPS_PER_BLOCK):
            if i & dist == 0:
                j = i + dist
                new[i] = jnp.where(upper, pltpu.roll(vs[j], SSM_GROUP * dist, 1), vs[i])
                new[j] = jnp.where(upper, vs[j], pltpu.roll(vs[i], LANES - SSM_GROUP * dist, 1))
        vs = new
    return vs


def _tokens_to_chunks(tok_ref, chunk_ref, n_chunks):
    for blk in range(tok_ref.shape[0]):
        for half in range(CHUNK_WIDTH // LANES):
            by_token = [tok_ref[blk, pl.ds(half * GROUPS_PER_BLOCK + t, n_chunks, stride=SSM_CHUNK), :]
                        for t in range(GROUPS_PER_BLOCK)]
            for g, val in enumerate(_granule_transpose(by_token)):
                chunk_ref[blk * GROUPS_PER_BLOCK + g, :, half * LANES:(half + 1) * LANES] = val


def _chunks_to_tokens(chunk_ref, tok_ref, n_chunks):
    for blk in range(tok_ref.shape[0]):
        for half in range(CHUNK_WIDTH // LANES):
            by_group = [chunk_ref[blk * GROUPS_PER_BLOCK + g, :, half * LANES:(half + 1) * LANES]
                        for g in range(GROUPS_PER_BLOCK)]
            for t, val in enumerate(_granule_transpose(by_group)):
                tok_ref[blk, pl.ds(half * GROUPS_PER_BLOCK + t, n_chunks, stride=SSM_CHUNK), :] = val


def _inproj_kernel(x_ref, g_ref, w_ref, q_ref, k_ref, v_ref, uc_ref, u_tok, *, aw):
    h = _rms(x_ref[...], g_ref[...]).astype(BF16)
    scale = HEAD_DIM ** -0.5
    q_ref[...] = (jnp.dot(h, w_ref[:, 0:aw], preferred_element_type=F32) * scale).astype(BF16)
    k_ref[...] = jnp.dot(h, w_ref[:, aw:2 * aw], preferred_element_type=F32).astype(BF16)
    v_ref[...] = jnp.dot(h, w_ref[:, 2 * aw:3 * aw], preferred_element_type=F32).astype(BF16)
    u = jnp.dot(h, w_ref[:, 3 * aw:], preferred_element_type=F32)
    for blk in range(u_tok.shape[0]):
        u_tok[blk] = u[:, blk * LANES:(blk + 1) * LANES]
    _tokens_to_chunks(u_tok, uc_ref, x_ref.shape[0] // SSM_CHUNK)


def _inproj(x, g, w_bf16, aw):
    t, d = x.shape
    n = w_bf16.shape[1]
    sw = n - 3 * aw
    tm = TOKEN_TILE
    groups, tile_chunks = sw // SSM_GROUP, tm // SSM_CHUNK
    row = lambda i: (i, 0)
    fixed = lambda i: (0, 0)
    return pl.pallas_call(
        functools.partial(_inproj_kernel, aw=aw),
        grid=(t // tm,),
        in_specs=[pl.BlockSpec((tm, d), row), pl.BlockSpec((1, d), fixed), pl.BlockSpec((d, n), fixed)],
        out_specs=[pl.BlockSpec((tm, aw), row), pl.BlockSpec((tm, aw), row), pl.BlockSpec((tm, aw), row),
                   pl.BlockSpec((groups, tile_chunks, CHUNK_WIDTH), lambda i: (0, i, 0))],
        out_shape=[jax.ShapeDtypeStruct((t, aw), BF16)] * 3
        + [jax.ShapeDtypeStruct((groups, t // SSM_CHUNK, CHUNK_WIDTH), F32)],
        scratch_shapes=[pltpu.VMEM((sw // LANES, tm, LANES), F32)],
        compiler_params=_cparams(("parallel",)),
        name="inproj",
    )(x, g.reshape(1, d), w_bf16)


def _t5_bucket(rel):
    nb = N_REL_BUCKETS // 2
    max_exact = nb // 2
    ret = (rel > 0).astype(np.int32) * nb
    n = np.abs(rel)
    large = max_exact + (np.log(np.maximum(n, 1) / max_exact)
                         / np.log(REL_MAX_DISTANCE / max_exact) * (nb - max_exact)).astype(np.int32)
    large = np.minimum(large, nb - 1)
    return (ret + np.where(n < max_exact, n, large)).astype(np.int32)


def _key_window(sub_len):
    return min(2 * Q_BLOCK, sub_len)


def _bias_tiles(rel_bias, dilation, sub_len):
    w = _key_window(sub_len)
    offsets = (0, -BAND_HALF, -2 * BAND_HALF) if sub_len > Q_BLOCK else (0,)
    heads = rel_bias.shape[1]
    span = Q_BLOCK + w
    k = np.arange(span)
    tiles = []
    for off in offsets:
        rel = np.where(k < w, k, k - span) + off
        line = rel_bias[jnp.asarray(_t5_bucket(rel * dilation))].astype(F32)
        line = jnp.where(jnp.asarray(np.abs(rel) <= BAND_HALF)[:, None], line, MASK_VALUE)
        rows = jnp.tile(line, (Q_BLOCK, 1))[:Q_BLOCK * (span - 1)].reshape(Q_BLOCK, span - 1, heads)
        tiles.append(jnp.transpose(rows[:, :w], (2, 0, 1)))
    return jnp.stack(tiles)


def _attn_tile(qb, kb, vb, bias):
    lane = lax.broadcasted_iota(jnp.int32, qb.shape, 1)
    zero = jnp.zeros_like(qb)
    q2 = jnp.concatenate([jnp.where(lane < HEAD_DIM, qb, zero), jnp.where(lane >= HEAD_DIM, qb, zero)], axis=0)
    s = lax.dot_general(q2, kb, (((1,), (1,)), ((), ())), preferred_element_type=F32) + bias
    m = jnp.max(s, axis=1, keepdims=True)
    p = jnp.exp(s - m)
    l = jnp.sum(p, axis=1, keepdims=True)
    o = jnp.dot(p.astype(BF16), vb, preferred_element_type=F32)
    first = lax.broadcasted_iota(jnp.int32, (Q_BLOCK, LANES), 1) < HEAD_DIM
    return (jnp.where(first, o[:Q_BLOCK], o[Q_BLOCK:]), jnp.where(first, m[:Q_BLOCK], m[Q_BLOCK:]),
            jnp.where(first, l[:Q_BLOCK], l[Q_BLOCK:]))


ATTN_UNROLL = 8
ATTN_STAGE_ROWS = 2048
COPY_ROWS = 512


def _attn_kernel(q_ref, k_ref, v_ref, b1_ref, b4_ref, b16_ref, o_ref,
                 stage, qd, kd, vd, m_acc, l_acc, *, seq):
    for dil, bias_ref in ((DILATIONS[2], b16_ref), (DILATIONS[1], b4_ref), (DILATIONS[0], b1_ref)):
        first_branch = dil == DILATIONS[2]
        sub_len = seq // dil
        n_blocks = sub_len // Q_BLOCK
        assert n_blocks & (n_blocks - 1) == 0 and (dil * n_blocks) % ATTN_UNROLL == 0
        w = _key_window(sub_len)

        if dil == 1:
            qs, ks, vs = q_ref, k_ref, v_ref
        else:
            seg = min(ATTN_STAGE_ROWS, seq)
            per = seg // dil

            def segment(sg, carry, dil=dil, sub_len=sub_len, seg=seg, per=per):
                base = pl.multiple_of(sg * seg, seg)
                for idx, src in enumerate((q_ref, k_ref, v_ref)):
                    for c in range(seg // COPY_ROWS):
                        stage[idx, c * COPY_ROWS:(c + 1) * COPY_ROWS, :] = (
                            src[pl.ds(base + c * COPY_ROWS, COPY_ROWS), :].astype(F32))

                def residue(r, inner):
                    dst = pl.ds(pl.multiple_of(r * sub_len + sg * per, Q_BLOCK), per)
                    for idx, dst_ref in enumerate((qd, kd, vd)):
                        dst_ref[dst, :] = stage[idx, pl.ds(r, per, stride=dil), :].astype(BF16)
                    return inner

                lax.fori_loop(0, dil, residue, 0)
                return carry

            lax.fori_loop(0, seq // seg, segment, 0)
            qs, ks, vs = qd, kd, vd

        def trip(j, carry, dil=dil, bias_ref=bias_ref, sub_len=sub_len, n_blocks=n_blocks, w=w,
                 qs=qs, ks=ks, vs=vs, first_branch=first_branch):
            for uu in range(ATTN_UNROLL):
                item = j * ATTN_UNROLL + uu
                if n_blocks == 1:
                    r, l0, ws, var = item, 0, 0, 0
                else:
                    r = jnp.right_shift(item, n_blocks.bit_length() - 1) if dil > 1 else 0
                    i = jnp.bitwise_and(item, n_blocks - 1) if dil > 1 else item
                    l0 = i * Q_BLOCK
                    ws = jnp.clip(l0 - BAND_HALF, 0, sub_len - w)
                    var = jnp.where(i == 0, 0, jnp.where(i == n_blocks - 1, 2, 1))
                base = r * sub_len
                qb = qs[pl.ds(pl.multiple_of(base + l0, Q_BLOCK), Q_BLOCK), :]
                kb = ks[pl.ds(pl.multiple_of(base + ws, BAND_HALF), w), :]
                vb = vs[pl.ds(pl.multiple_of(base + ws, BAND_HALF), w), :]
                o, m, l = _attn_tile(qb, kb, vb, bias_ref[var].reshape(HEADS_PER_BLOCK * Q_BLOCK, w))
                if dil == 1:
                    rows = pl.ds(pl.multiple_of(l0, Q_BLOCK), Q_BLOCK)
                else:
                    rows = pl.ds(r + dil * l0, Q_BLOCK, stride=dil)
                if first_branch:
                    o_ref[rows, :] = o
                    m_acc[rows, :] = m
                    l_acc[rows, :] = l
                else:
                    m_old = m_acc[rows, :]
                    m_new = jnp.maximum(m_old, m)
                    a = jnp.exp(m_old - m_new)
                    b = jnp.exp(m - m_new)
                    o_ref[rows, :] = a * o_ref[rows, :] + b * o
                    l_acc[rows, :] = a * l_acc[rows, :] + b * l
                    m_acc[rows, :] = m_new
            return carry

        lax.fori_loop(0, dil * n_blocks // ATTN_UNROLL, trip, 0)

    def finish(c, carry):
        rows = pl.ds(pl.multiple_of(c * COPY_ROWS, COPY_ROWS), COPY_ROWS)
        o_ref[rows, :] = o_ref[rows, :] / l_acc[rows, :]
        return carry

    lax.fori_loop(0, seq // COPY_ROWS, finish, 0)


def _attention(q, k, v, bias_tiles):
    b, s, a = q.shape
    n_pairs = a // LANES
    assert s % (DILATIONS[-1] * Q_BLOCK) == 0
    seq_spec = pl.BlockSpec((None, s, LANES), lambda bi, hp: (bi, 0, hp))
    bias_specs = [pl.BlockSpec((t.shape[0], HEADS_PER_BLOCK) + t.shape[2:], lambda bi, hp: (0, hp, 0, 0))
                  for t in bias_tiles]
    return pl.pallas_call(
        functools.partial(_attn_kernel, seq=s),
        grid=(b, n_pairs),
        in_specs=[seq_spec, seq_spec, seq_spec] + bias_specs,
        out_specs=seq_spec,
        out_shape=jax.ShapeDtypeStruct((b, s, a), F32),
        scratch_shapes=[pltpu.VMEM((3, min(ATTN_STAGE_ROWS, s), LANES), F32)]
        + [pltpu.VMEM((s, LANES), BF16)] * 3
        + [pltpu.VMEM((s, LANES), F32)] * 2,
        compiler_params=_cparams(("parallel", "parallel")),
        name="dilated_attn",
    )(q, k, v, *bias_tiles)


def _ssm_matrices(lam_re, lam_im, log_step, b_re, b_im, c_re, c_im):
    hp = lax.Precision.HIGHEST
    t_len, n_ch = SSM_CHUNK, SSM_GROUP
    step = jnp.exp(log_step.astype(F32))[..., None]
    lr, li = lam_re.astype(F32), lam_im.astype(F32)
    zr, zi = lr * step, li * step
    tau = jnp.arange(t_len + 1, dtype=F32)[:, None, None, None]
    mag = jnp.exp(tau * zr)
    pr, pi = mag * jnp.cos(tau * zi), mag * jnp.sin(tau * zi)
    nr, ni = pr[1] - 1.0, pi[1]
    den = lr * lr + li * li
    fr, fi = (nr * lr + ni * li) / den, (ni * lr - nr * li) / den
    bbr = fr[..., None] * b_re - fi[..., None] * b_im
    bbi = fr[..., None] * b_im + fi[..., None] * b_re
    cpr = c_re[None] * pr[:, :, :, None, :] - c_im[None] * pi[:, :, :, None, :]
    cpi = c_re[None] * pi[:, :, :, None, :] + c_im[None] * pr[:, :, :, None, :]
    kern = (jnp.einsum('tdgcn,dgnk->tdgck', cpr, bbr, precision=hp)
            - jnp.einsum('tdgcn,dgnk->tdgck', cpi, bbi, precision=hp))
    g = lam_re.shape[1]

    def toeplitz(k_dir, causal):
        taps = k_dir[:t_len]
        line = jnp.concatenate([taps, jnp.zeros_like(taps)], axis=0)
        m = jnp.tile(line, (t_len, 1, 1, 1))[:t_len * (2 * t_len - 1)]
        m = m.reshape((t_len, 2 * t_len - 1) + taps.shape[1:])[:, :t_len]
        if not causal:
            m = jnp.swapaxes(m, 0, 1)
        return jnp.transpose(m, (2, 0, 4, 1, 3)).reshape(g, CHUNK_WIDTH, CHUNK_WIDTH)

    m_sum = toeplitz(kern[:, 0], True) + toeplitz(kern[:, 1], False)

    def state_in(direction, p_r, p_i):
        p_r, p_i = p_r[:, direction], p_i[:, direction]
        re = p_r[..., None] * bbr[direction][None] - p_i[..., None] * bbi[direction][None]
        im = p_r[..., None] * bbi[direction][None] + p_i[..., None] * bbr[direction][None]
        re = jnp.transpose(re, (1, 0, 3, 2)).reshape(g, CHUNK_WIDTH, SSM_STATE)
        im = jnp.transpose(im, (1, 0, 3, 2)).reshape(g, CHUNK_WIDTH, SSM_STATE)
        return jnp.concatenate([re, im, im, re], axis=-1)

    w1 = jnp.concatenate([m_sum, state_in(0, pr[:t_len][::-1], pi[:t_len][::-1]),
                          state_in(1, pr[:t_len], pi[:t_len])], axis=-1)

    def state_out(direction, c_r, c_i):
        c_r, c_i = c_r[:, direction], c_i[:, direction]
        re = jnp.transpose(c_r, (1, 3, 0, 2)).reshape(g, SSM_STATE, CHUNK_WIDTH)
        im = jnp.transpose(-c_i, (1, 3, 0, 2)).reshape(g, SSM_STATE, CHUNK_WIDTH)
        return jnp.concatenate([re, im], axis=1)

    w2 = jnp.concatenate([state_out(0, cpr[1:], cpi[1:]),
                          state_out(1, cpr[1:][::-1], cpi[1:][::-1])], axis=1)
    ar, ai = pr[t_len], pi[t_len]
    coef = jnp.stack([jnp.concatenate([ar[0], ar[0]], -1), jnp.concatenate([-ai[0], ai[0]], -1),
                      jnp.concatenate([ar[1], ar[1]], -1), jnp.concatenate([-ai[1], ai[1]], -1)], axis=1)
    return w1.astype(BF16), w2.astype(BF16), coef


SUBLANES = 8
SCAN_PITCH_PAD = 8


def _ssm_kernel(u_ref, w1_ref, w2_ref, coef_ref, d_ref, y_ref, zf_loc, wf_loc, zb_loc, wb_loc,
                zf_in, zb_in, *, n_chunks, batch):
    rows_total = batch * n_chunks
    rc = min(512, rows_total)
    piece = min(rc, n_chunks)
    pitch = n_chunks + SCAN_PITCH_PAD
    sw = STATE_WIDTH
    octets = batch // SUBLANES

    def scratch_rows(c, p):
        first = c * rc + p * piece
        if piece == n_chunks:
            return pl.ds(pl.multiple_of((first // n_chunks) * pitch, SUBLANES), piece)
        seq = first // n_chunks
        return pl.ds(pl.multiple_of(seq * pitch + (first - seq * n_chunks), SUBLANES), piece)

    def local(c, carry):
        rows = pl.ds(pl.multiple_of(c * rc, rc), rc)
        u = u_ref[rows, :]
        res = jnp.dot(u.astype(BF16), w1_ref[...], preferred_element_type=F32)
        y_ref[rows, :] = res[:, :CHUNK_WIDTH] + u * d_ref[...]
        for p in range(rc // piece):
            srows = scratch_rows(c, p)
            for idx, dst in enumerate((zf_loc, wf_loc, zb_loc, wb_loc)):
                dst[srows, :] = res[p * piece:(p + 1) * piece,
                                    CHUNK_WIDTH + idx * sw:CHUNK_WIDTH + (idx + 1) * sw]
        return carry

    lax.fori_loop(0, rows_total // rc, local, 0)

    ar_f = jnp.broadcast_to(coef_ref[0:1, :], (SUBLANES, sw))
    ai_f = jnp.broadcast_to(coef_ref[1:2, :], (SUBLANES, sw))
    ar_b = jnp.broadcast_to(coef_ref[2:3, :], (SUBLANES, sw))
    ai_b = jnp.broadcast_to(coef_ref[3:4, :], (SUBLANES, sw))

    def step(k, state):
        new = []
        for oc in range(octets):
            zf, wf, zb, wb = state[4 * oc:4 * oc + 4]
            rf = pl.ds(oc * SUBLANES * pitch + k, SUBLANES, stride=pitch)
            rb = pl.ds(oc * SUBLANES * pitch + (n_chunks - 1 - k), SUBLANES, stride=pitch)
            zf_in[rf, :] = zf
            zb_in[rb, :] = zb
            new += [ar_f * zf + ai_f * wf + zf_loc[rf, :], ar_f * wf - ai_f * zf + wf_loc[rf, :],
                    ar_b * zb + ai_b * wb + zb_loc[rb, :], ar_b * wb - ai_b * zb + wb_loc[rb, :]]
        return tuple(new)

    zero = jnp.zeros((SUBLANES, sw), F32)
    lax.fori_loop(0, n_chunks, step, (zero,) * (4 * octets), unroll=4)

    def carried(c, carry):
        rows = pl.ds(pl.multiple_of(c * rc, rc), rc)
        pieces = [scratch_rows(c, p) for p in range(rc // piece)]
        zf = jnp.concatenate([zf_in[srows, :] for srows in pieces], axis=0)
        zb = jnp.concatenate([zb_in[srows, :] for srows in pieces], axis=0)
        y_ref[rows, :] += (jnp.dot(zf.astype(BF16), w2_ref[0:sw, :], preferred_element_type=F32)
                           + jnp.dot(zb.astype(BF16), w2_ref[sw:, :], preferred_element_type=F32))
        return carry

    lax.fori_loop(0, rows_total // rc, carried, 0)


def _ssm_scan(uc, w1, w2, coef, d_skip, batch):
    g, rows, _ = uc.shape
    n_chunks = rows // batch
    assert batch % SUBLANES == 0 and n_chunks % SUBLANES == 0
    d_rows = jnp.tile(d_skip.astype(F32).reshape(g, 1, SSM_GROUP), (1, SSM_CHUNK, 1)).reshape(g, 1, CHUNK_WIDTH)
    per_group = lambda shape: pl.BlockSpec((None,) + shape, lambda gi: (gi,) + (0,) * len(shape))
    scratch_rows = batch * (n_chunks + SCAN_PITCH_PAD)
    return pl.pallas_call(
        functools.partial(_ssm_kernel, n_chunks=n_chunks, batch=batch),
        grid=(g,),
        in_specs=[per_group((rows, CHUNK_WIDTH)), per_group(w1.shape[1:]), per_group(w2.shape[1:]),
                  per_group(coef.shape[1:]), per_group((1, CHUNK_WIDTH))],
        out_specs=per_group((rows, CHUNK_WIDTH)),
        out_shape=jax.ShapeDtypeStruct((g, rows, CHUNK_WIDTH), F32),
        scratch_shapes=[pltpu.VMEM((scratch_rows, STATE_WIDTH), F32)] * 6,
        compiler_params=_cparams(("parallel",)),
        name="s5_scan",
    )(uc, w1, w2, coef, d_rows)


def _mixout_kernel(x_ref, a_ref, yc_ref, gw_ref, gb_ref, ga_ref, gs_ref, wo_ref, o_ref, y_tok, *, aw):
    _chunks_to_tokens(yc_ref, y_tok, x_ref.shape[0] // SSM_CHUNK)
    y = jnp.concatenate([y_tok[blk] for blk in range(y_tok.shape[0])], axis=1)
    yg = 0.5 * y * (1.0 + jnp.tanh(math.sqrt(2.0 / math.pi) * (y + 0.044715 * (y * y * y))))
    gate = _sigmoid(jnp.dot(yg.astype(BF16), gw_ref[...], preferred_element_type=F32) + gb_ref[...])
    an = _rms(a_ref[...], ga_ref[...]).astype(BF16)
    sn = _rms(yg * gate, gs_ref[...]).astype(BF16)
    o_ref[...] = (x_ref[...] + jnp.dot(an, wo_ref[0:aw, :], preferred_element_type=F32)
                  + jnp.dot(sn, wo_ref[aw:, :], preferred_element_type=F32))


def _mixout(x, attn, yc, glu_w, glu_b, g_attn, g_ssm, w_out):
    t, d = x.shape
    aw = attn.shape[1]
    groups = yc.shape[0]
    sw = groups * SSM_GROUP
    tm = TOKEN_TILE
    row = lambda i: (i, 0)
    fixed = lambda i: (0, 0)
    return pl.pallas_call(
        functools.partial(_mixout_kernel, aw=aw),
        grid=(t // tm,),
        in_specs=[pl.BlockSpec((tm, d), row), pl.BlockSpec((tm, aw), row),
                  pl.BlockSpec((groups, tm // SSM_CHUNK, CHUNK_WIDTH), lambda i: (0, i, 0)),
                  pl.BlockSpec((sw, sw), fixed), pl.BlockSpec((1, sw), fixed), pl.BlockSpec((1, aw), fixed),
                  pl.BlockSpec((1, sw), fixed), pl.BlockSpec((d, d), fixed)],
        out_specs=pl.BlockSpec((tm, d), row),
        out_shape=jax.ShapeDtypeStruct((t, d), F32),
        scratch_shapes=[pltpu.VMEM((sw // LANES, tm, LANES), F32)],
        compiler_params=_cparams(("parallel",)),
        name="mixer_out",
    )(x, attn, yc, glu_w, glu_b.reshape(1, sw), g_attn.reshape(1, aw), g_ssm.reshape(1, sw), w_out)


MXU_WIDTH = 256
FF_CHUNK_TILES = 6


def _ff_chunks(d_ff):
    assert d_ff % MXU_WIDTH == 0
    tiles = d_ff // MXU_WIDTH
    n = -(-tiles // FF_CHUNK_TILES)
    sizes = [(tiles // n + (1 if c < tiles % n else 0)) * MXU_WIDTH for c in range(n)]
    return [(sum(sizes[:c]), sizes[c]) for c in range(n)]


def _swiglu(h, w1_ref, w3_ref, w2_ref):
    d_ff = w1_ref.shape[1]
    acc = None
    for start, size in _ff_chunks(d_ff):
        a = jnp.dot(h, w1_ref[:, start:start + size], preferred_element_type=F32)
        b = jnp.dot(h, w3_ref[:, start:start + size], preferred_element_type=F32)
        act = (a * _sigmoid(a) * b).astype(BF16)
        part = jnp.dot(act, w2_ref[start:start + size, :], preferred_element_type=F32)
        acc = part if acc is None else acc + part
    return acc


def _ffn_kernel(x_ref, g_ref, w1_ref, w3_ref, w2_ref, o_ref):
    x = x_ref[...]
    h = _rms(x, g_ref[...]).astype(BF16)
    o_ref[...] = x + _swiglu(h, w1_ref, w3_ref, w2_ref)


def _dense_ffn(x, g, w1, w3, w2):
    t, d = x.shape
    d_ff = w1.shape[1]
    tm = TOKEN_TILE
    row = lambda i: (i, 0)
    fixed = lambda i: (0, 0)
    resident = lambda shape: pl.BlockSpec(shape, fixed, pipeline_mode=pl.Buffered(1))
    return pl.pallas_call(
        _ffn_kernel,
        grid=(t // tm,),
        in_specs=[pl.BlockSpec((tm, d), row), pl.BlockSpec((1, d), fixed),
                  resident((d, d_ff)), resident((d, d_ff)), resident((d_ff, d))],
        out_specs=pl.BlockSpec((tm, d), row),
        out_shape=jax.ShapeDtypeStruct((t, d), F32),
        compiler_params=_cparams(("parallel",)),
        name="dense_ffn",
    )(x, g.reshape(1, d), w1, w3, w2)


def _router_kernel(x_ref, g_ref, r_ref, idx_ref, wt_ref, *, n_experts):
    h = _rms(x_ref[...], g_ref[...])
    r = r_ref[...]
    h_hi, r_hi = h.astype(BF16), r.astype(BF16)
    h_lo = (h - h_hi.astype(F32)).astype(BF16)
    r_lo = (r - r_hi.astype(F32)).astype(BF16)
    logits = (jnp.dot(h_hi, r_hi, preferred_element_type=F32) + jnp.dot(h_hi, r_lo, preferred_element_type=F32)
              + jnp.dot(h_lo, r_hi, preferred_element_type=F32))
    lane = lax.broadcasted_iota(jnp.int32, logits.shape, 1).astype(F32)
    logits = jnp.where(lane < n_experts, logits, -jnp.inf)
    v1 = jnp.max(logits, axis=1, keepdims=True)
    i1 = jnp.min(jnp.where(logits == v1, lane, float(LANES)), axis=1, keepdims=True)
    rest = jnp.where(lane == i1, -jnp.inf, logits)
    v2 = jnp.max(rest, axis=1, keepdims=True)
    i2 = jnp.min(jnp.where(rest == v2, lane, float(LANES)), axis=1, keepdims=True)
    e = jnp.exp(v2 - v1)
    w1 = 1.0 / (1.0 + e)
    w2 = e / (1.0 + e)
    idx_ref[...] = jnp.where(lane == 0.0, i1, jnp.where(lane == 1.0, i2, 0.0))[:, :8].astype(jnp.int32)
    wt_ref[...] = jnp.where(lane == 0.0, w1, jnp.where(lane == 1.0, w2, 0.0))[:, :8]


def _router(x, g, router):
    t, d = x.shape
    n_experts = router.shape[1]
    tm = TOKEN_TILE
    r_pad = jnp.zeros((d, LANES), F32).at[:, :n_experts].set(router.astype(F32))
    row = lambda i: (i, 0)
    fixed = lambda i: (0, 0)
    return pl.pallas_call(
        functools.partial(_router_kernel, n_experts=n_experts),
        grid=(t // tm,),
        in_specs=[pl.BlockSpec((tm, d), row), pl.BlockSpec((1, d), fixed), pl.BlockSpec((d, LANES), fixed)],
        out_specs=[pl.BlockSpec((tm, 8), row), pl.BlockSpec((tm, 8), row)],
        out_shape=[jax.ShapeDtypeStruct((t, 8), jnp.int32), jax.ShapeDtypeStruct((t, 8), F32)],
        compiler_params=_cparams(("parallel",)),
        name="router",
    )(x, g.reshape(1, d), r_pad)


def _row_copy(src, src_row, dst, dst_row, sem):
    return pltpu.make_async_copy(src.at[pl.ds(src_row, 1), :], dst.at[pl.ds(dst_row, 1), :], sem)


def _moe_kernel(te_ref, src0_ref, srcn_ref, dst_ref, x_hbm, g_ref, w1_ref, w3_ref, w2_ref, o_hbm,
                xbuf, ybuf, gsem, ssem, *, tm):
    del te_ref
    i = pl.program_id(0)
    n_tiles = pl.num_programs(0)
    slot = lax.rem(i, 2)
    other = 1 - slot

    def start_gather(idx_ref, s):
        for j in range(tm):
            _row_copy(x_hbm, idx_ref[0, j], xbuf.at[s], j, gsem.at[s]).start()

    def wait_gather(s):
        for j in range(tm):
            _row_copy(x_hbm, 0, xbuf.at[s], j, gsem.at[s]).wait()

    def wait_scatter(s):
        for j in range(tm):
            _row_copy(ybuf.at[s], j, o_hbm, 0, ssem.at[s]).wait()

    @pl.when(i == 0)
    def _():
        start_gather(src0_ref, 0)

    @pl.when(i >= 2)
    def _():
        wait_scatter(slot)

    wait_gather(slot)
    start_gather(srcn_ref, other)
    h = _rms(xbuf[slot], g_ref[...]).astype(BF16)
    ybuf[slot] = _swiglu(h, w1_ref, w3_ref, w2_ref)
    for j in range(tm):
        _row_copy(ybuf.at[slot], j, o_hbm, dst_ref[0, j], ssem.at[slot]).start()

    @pl.when(i == n_tiles - 1)
    def _():
        wait_gather(other)
        wait_scatter(slot)

        @pl.when(n_tiles >= 2)
        def _():
            wait_scatter(other)


def _moe_plan(idx, n_experts, tm):
    i32 = jnp.int32
    t = idx.shape[0]
    n_assign = TOP_K * t
    pad = n_experts * tm
    n_rows = n_assign + pad
    n_tiles = n_rows // tm
    flat = idx.reshape(-1)
    experts = jnp.arange(n_experts, dtype=i32)
    keys = jnp.sort(flat * n_assign + jnp.arange(n_assign, dtype=i32))
    counts = jnp.sum((flat[:, None] == experts[None, :]).astype(i32), axis=0)
    padded = (counts + tm - 1) // tm * tm
    ends = jnp.cumsum(padded)
    off = jnp.cumsum(counts) - counts
    poff = ends - padded
    rows = jnp.arange(n_rows, dtype=i32)
    e_row = jnp.minimum(jnp.sum((rows[:, None] >= ends[None, :]).astype(i32), axis=1), n_experts - 1)
    onehot = (e_row[:, None] == experts[None, :]).astype(i32)
    rel = rows - jnp.sum(onehot * poff[None, :], axis=1)
    count_row = jnp.sum(onehot * counts[None, :], axis=1)
    valid = rel < count_row
    keys_padded = jnp.concatenate([keys, jnp.zeros((pad,), i32)])
    key_row = jnp.zeros((n_rows,), i32)
    for e in range(n_experts):
        key_row = jnp.where(e_row == e, jnp.roll(keys_padded, poff[e] - off[e]), key_row)
    assign = key_row - e_row * n_assign
    token, slot = assign // TOP_K, assign % TOP_K
    valid_before = jnp.sum(onehot * off[None, :], axis=1) + jnp.minimum(rel, count_row)
    src = jnp.where(valid, token, 0)
    dst = jnp.where(valid, slot * t + token, n_assign + rows - valid_before)
    tile_e = e_row.reshape(n_tiles, tm)[:, 0]
    return src.reshape(n_tiles, 1, tm), dst.reshape(n_tiles, 1, tm), tile_e


def _moe_experts(x, g, w1, w3, w2, src, dst, tile_e):
    t, d = x.shape
    d_ff = w1.shape[2]
    n_tiles, _, tm = src.shape
    smem_row = lambda fn: pl.BlockSpec((None, 1, tm), fn, memory_space=pltpu.SMEM)
    expert = lambda shape: pl.BlockSpec((None,) + shape, lambda i, te: (te[i], 0, 0))
    grid_spec = pltpu.PrefetchScalarGridSpec(
        num_scalar_prefetch=1,
        grid=(n_tiles,),
        in_specs=[smem_row(lambda i, te: (0, 0, 0)),
                  smem_row(lambda i, te: (jnp.minimum(i + 1, n_tiles - 1), 0, 0)),
                  smem_row(lambda i, te: (i, 0, 0)),
                  pl.BlockSpec(memory_space=pl.ANY),
                  pl.BlockSpec((1, d), lambda i, te: (0, 0)),
                  expert((d, d_ff)), expert((d, d_ff)), expert((d_ff, d))],
        out_specs=pl.BlockSpec(memory_space=pl.ANY),
        scratch_shapes=[pltpu.VMEM((2, tm, d), F32), pltpu.VMEM((2, tm, d), F32),
                        pltpu.SemaphoreType.DMA((2,)), pltpu.SemaphoreType.DMA((2,))],
    )
    return pl.pallas_call(
        functools.partial(_moe_kernel, tm=tm),
        grid_spec=grid_spec,
        out_shape=jax.ShapeDtypeStruct((n_tiles * tm, d), F32),
        compiler_params=_cparams(("arbitrary",)),
        name="moe_experts",
    )(tile_e, src, src, dst, x, g.reshape(1, d), w1, w3, w2)


def _combine_kernel(x_ref, y0_ref, y1_ref, wt_ref, g_ref, o_ref, *, final_norm):
    wt = wt_ref[...]
    out = x_ref[...] + wt[:, 0:1] * y0_ref[...] + wt[:, 1:2] * y1_ref[...]
    if final_norm:
        out = _rms(out, g_ref[...])
    o_ref[...] = out


def _moe_combine(x, y2, wt, g_final, final_norm):
    t, d = x.shape
    tm = TOKEN_TILE
    row = lambda i: (i, 0)
    return pl.pallas_call(
        functools.partial(_combine_kernel, final_norm=final_norm),
        grid=(t // tm,),
        in_specs=[pl.BlockSpec((tm, d), row), pl.BlockSpec((tm, d), row),
                  pl.BlockSpec((tm, d), lambda i: (i + t // tm, 0)),
                  pl.BlockSpec((tm, 8), row), pl.BlockSpec((1, d), lambda i: (0, 0))],
        out_specs=pl.BlockSpec((tm, d), row),
        out_shape=jax.ShapeDtypeStruct((t, d), F32),
        compiler_params=_cparams(("parallel",)),
        name="moe_combine",
    )(x, y2, y2, wt, g_final.reshape(1, d))


def _norm_kernel(x_ref, g_ref, o_ref):
    o_ref[...] = _rms(x_ref[...], g_ref[...])


def _final_norm(x, g):
    t, d = x.shape
    tm = TOKEN_TILE
    return pl.pallas_call(
        _norm_kernel,
        grid=(t // tm,),
        in_specs=[pl.BlockSpec((tm, d), lambda i: (i, 0)), pl.BlockSpec((1, d), lambda i: (0, 0))],
        out_specs=pl.BlockSpec((tm, d), lambda i: (i, 0)),
        out_shape=jax.ShapeDtypeStruct((t, d), F32),
        compiler_params=_cparams(("parallel",)),
        name="final_norm",
    )(x, g.reshape(1, d))


def _trunk(x, p):
    b, s, d = x.shape
    t = b * s
    aw = p['attn_out_norm'].shape[1]
    depth = p['w_in'].shape[0]
    xt = x.reshape(t, d)
    normed = False
    for layer in range(depth):
        q, k, v, uc = _inproj(xt, p['norm_mix'][layer], p['w_in'][layer], aw)
        shape3 = lambda z: z.reshape(b, s, z.shape[1])
        attn = _attention(shape3(q), shape3(k), shape3(v), p['bias_tiles'][s]).reshape(t, aw)
        w1m, w2m, coef = p['ssm'][layer]
        yc = _ssm_scan(uc, w1m, w2m, coef, p['ssm_d'][layer], b)
        xt = _mixout(xt, attn, yc, p['glu_w'][layer], p['glu_b'][layer], p['attn_out_norm'][layer],
                     p['ssm_out_norm'][layer], p['w_out'][layer])
        g = p['norm_ffn'][layer]
        j = layer // 2
        if layer % 2 == 0:
            xt = _dense_ffn(xt, g, p['ffn_w1'][j], p['ffn_w3'][j], p['ffn_w2'][j])
        else:
            idx, wt = _router(xt, g, p['moe_router'][j])
            n_experts = p['moe_router'].shape[2]
            src, dst, tile_e = _moe_plan(idx[:, :TOP_K], n_experts, MOE_TILE)
            y2 = _moe_experts(xt, g, p['moe_w1'][j], p['moe_w3'][j], p['moe_w2'][j], src, dst, tile_e)
            last = layer == depth - 1
            xt = _moe_combine(xt, y2, wt, p['final_norm'], last)
            normed = last
    if not normed:
        xt = _final_norm(xt, p['final_norm'])
    return xt.reshape(b, s, d)


def kernel(x_prompt, x_sample, rel_bias, norm_mix, w_in, ssm_lambda_re, ssm_lambda_im, ssm_log_step, ssm_b_re, ssm_b_im, ssm_c_re, ssm_c_im, ssm_d, glu_w, glu_b, attn_out_norm, ssm_out_norm, w_out, norm_ffn, ffn_w1, ffn_w3, ffn_w2, moe_router, moe_w1, moe_w3, moe_w2, final_norm):
    depth = w_in.shape[0]
    bias_tiles = {}
    for x in (x_prompt, x_sample):
        s = x.shape[1]
        if s not in bias_tiles:
            bias_tiles[s] = [_bias_tiles(rel_bias, dil, s // dil) for dil in DILATIONS]
    p = {
        'bias_tiles': bias_tiles,
        'norm_mix': norm_mix, 'w_in': w_in.astype(BF16),
        'ssm': [_ssm_matrices(ssm_lambda_re[l], ssm_lambda_im[l], ssm_log_step[l], ssm_b_re[l], ssm_b_im[l],
                              ssm_c_re[l], ssm_c_im[l]) for l in range(depth)],
        'ssm_d': ssm_d, 'glu_w': glu_w.astype(BF16), 'glu_b': glu_b,
        'attn_out_norm': attn_out_norm, 'ssm_out_norm': ssm_out_norm, 'w_out': w_out.astype(BF16),
        'norm_ffn': norm_ffn,
        'ffn_w1': ffn_w1.astype(BF16), 'ffn_w3': ffn_w3.astype(BF16), 'ffn_w2': ffn_w2.astype(BF16),
        'moe_router': moe_router,
        'moe_w1': moe_w1.astype(BF16), 'moe_w3': moe_w3.astype(BF16), 'moe_w2': moe_w2.astype(BF16),
        'final_norm': final_norm,
    }
    return (_trunk(x_prompt, p), _trunk(x_sample, p))
```

```python
import functools
import math

import numpy as np
import jax
import jax.numpy as jnp
from jax import lax
from jax.experimental import pallas as pl
from jax.experimental.pallas import tpu as pltpu

F32 = jnp.float32
BF16 = jnp.bfloat16

LANES = 128
VMEM_LIMIT_BYTES = 56 * 1024 * 1024

HEAD_DIM = 64
HEADS_PER_BLOCK = LANES // HEAD_DIM
DILATIONS = (1, 4, 16)
BAND_HALF = 64
Q_BLOCK = 128
N_REL_BUCKETS = 32
REL_MAX_DISTANCE = 1024
SSM_GROUP = 16
SSM_STATE = 64
GROUPS_PER_BLOCK = LANES // SSM_GROUP
SSM_CHUNK = 16
CHUNK_WIDTH = SSM_CHUNK * SSM_GROUP
STATE_WIDTH = 2 * SSM_STATE
TOP_K = 2
NORM_EPS = 1e-6
MASK_VALUE = -1e30

TOKEN_TILE = 512
MOE_TILE = 1024


def _cparams(semantics):
    return pltpu.CompilerParams(dimension_semantics=semantics, vmem_limit_bytes=VMEM_LIMIT_BYTES)


def _rms(x, g):
    return x * lax.rsqrt(jnp.mean(x * x, axis=-1, keepdims=True) + NORM_EPS) * g


def _sigmoid(x):
    return 1.0 / (1.0 + jnp.exp(-x))


def _granule_transpose(vs):
    lane_granule = lax.broadcasted_iota(jnp.int32, vs[0].shape, 1) // SSM_GROUP
    for dist in (4, 2, 1):
        upper = (lane_granule & dist) != 0
        new = list(vs)
        for i in range(GROUPS_PER_BLOCK):
            if i & dist == 0:
                j = i + dist
                new[i] = jnp.where(upper, pltpu.roll(vs[j], SSM_GROUP * dist, 1), vs[i])
                new[j] = jnp.where(upper, vs[j], pltpu.roll(vs[i], LANES - SSM_GROUP * dist, 1))
        vs = new
    return vs


def _tokens_to_chunks(tok_ref, chunk_ref, n_chunks):
    for blk in range(tok_ref.shape[0]):
        for half in range(CHUNK_WIDTH // LANES):
            by_token = [tok_ref[blk, pl.ds(half * GROUPS_PER_BLOCK + t, n_chunks, stride=SSM_CHUNK), :]
                        for t in range(GROUPS_PER_BLOCK)]
            for g, val in enumerate(_granule_transpose(by_token)):
                chunk_ref[blk * GROUPS_PER_BLOCK + g, :, half * LANES:(half + 1) * LANES] = val


def _chunks_to_tokens(chunk_ref, tok_ref, n_chunks):
    for blk in range(tok_ref.shape[0]):
        for half in range(CHUNK_WIDTH // LANES):
            by_group = [chunk_ref[blk * GROUPS_PER_BLOCK + g, :, half * LANES:(half + 1) * LANES]
                        for g in range(GROUPS_PER_BLOCK)]
            for t, val in enumerate(_granule_transpose(by_group)):
                tok_ref[blk, pl.ds(half * GROUPS_PER_BLOCK + t, n_chunks, stride=SSM_CHUNK), :] = val


def _inproj_kernel(x_ref, g_ref, w_ref, q_ref, k_ref, v_ref, uc_ref, u_tok, *, aw):
    h = _rms(x_ref[...], g_ref[...]).astype(BF16)
    scale = HEAD_DIM ** -0.5
    q_ref[...] = (jnp.dot(h, w_ref[:, 0:aw], preferred_element_type=F32) * scale).astype(BF16)
    k_ref[...] = jnp.dot(h, w_ref[:, aw:2 * aw], preferred_element_type=F32).astype(BF16)
    v_ref[...] = jnp.dot(h, w_ref[:, 2 * aw:3 * aw], preferred_element_type=F32).astype(BF16)
    u = jnp.dot(h, w_ref[:, 3 * aw:], preferred_element_type=F32)
    for blk in range(u_tok.shape[0]):
        u_tok[blk] = u[:, blk * LANES:(blk + 1) * LANES]
    _tokens_to_chunks(u_tok, uc_ref, x_ref.shape[0] // SSM_CHUNK)


def _inproj(x, g, w_bf16, aw):
    t, d = x.shape
    n = w_bf16.shape[1]
    sw = n - 3 * aw
    tm = TOKEN_TILE
    groups, tile_chunks = sw // SSM_GROUP, tm // SSM_CHUNK
    row = lambda i: (i, 0)
    fixed = lambda i: (0, 0)
    return pl.pallas_call(
        functools.partial(_inproj_kernel, aw=aw),
        grid=(t // tm,),
        in_specs=[pl.BlockSpec((tm, d), row), pl.BlockSpec((1, d), fixed), pl.BlockSpec((d, n), fixed)],
        out_specs=[pl.BlockSpec((tm, aw), row), pl.BlockSpec((tm, aw), row), pl.BlockSpec((tm, aw), row),
                   pl.BlockSpec((groups, tile_chunks, CHUNK_WIDTH), lambda i: (0, i, 0))],
        out_shape=[jax.ShapeDtypeStruct((t, aw), BF16)] * 3
        + [jax.ShapeDtypeStruct((groups, t // SSM_CHUNK, CHUNK_WIDTH), F32)],
        scratch_shapes=[pltpu.VMEM((sw // LANES, tm, LANES), F32)],
        compiler_params=_cparams(("parallel",)),
        name="inproj",
    )(x, g.reshape(1, d), w_bf16)


def _t5_bucket(rel):
    nb = N_REL_BUCKETS // 2
    max_exact = nb // 2
    ret = (rel > 0).astype(np.int32) * nb
    n = np.abs(rel)
    large = max_exact + (np.log(np.maximum(n, 1) / max_exact)
                         / np.log(REL_MAX_DISTANCE / max_exact) * (nb - max_exact)).astype(np.int32)
    large = np.minimum(large, nb - 1)
    return (ret + np.where(n < max_exact, n, large)).astype(np.int32)


def _key_window(sub_len):
    return min(2 * Q_BLOCK, sub_len)


def _bias_tiles(rel_bias, dilation, sub_len):
    w = _key_window(sub_len)
    offsets = (0, -BAND_HALF, -2 * BAND_HALF) if sub_len > Q_BLOCK else (0,)
    heads = rel_bias.shape[1]
    span = Q_BLOCK + w
    k = np.arange(span)
    tiles = []
    for off in offsets:
        rel = np.where(k < w, k, k - span) + off
        line = rel_bias[jnp.asarray(_t5_bucket(rel * dilation))].astype(F32)
        line = jnp.where(jnp.asarray(np.abs(rel) <= BAND_HALF)[:, None], line, MASK_VALUE)
        rows = jnp.tile(line, (Q_BLOCK, 1))[:Q_BLOCK * (span - 1)].reshape(Q_BLOCK, span - 1, heads)
        tiles.append(jnp.transpose(rows[:, :w], (2, 0, 1)))
    return jnp.stack(tiles)


def _attn_tile(qb, kb, vb, bias):
    lane = lax.broadcasted_iota(jnp.int32, qb.shape, 1)
    zero = jnp.zeros_like(qb)
    q2 = jnp.concatenate([jnp.where(lane < HEAD_DIM, qb, zero), jnp.where(lane >= HEAD_DIM, qb, zero)], axis=0)
    s = lax.dot_general(q2, kb, (((1,), (1,)), ((), ())), preferred_element_type=F32) + bias
    m = jnp.max(s, axis=1, keepdims=True)
    p = jnp.exp(s - m)
    l = jnp.sum(p, axis=1, keepdims=True)
    o = jnp.dot(p.astype(BF16), vb, preferred_element_type=F32)
    first = lax.broadcasted_iota(jnp.int32, (Q_BLOCK, LANES), 1) < HEAD_DIM
    return (jnp.where(first, o[:Q_BLOCK], o[Q_BLOCK:]), jnp.where(first, m[:Q_BLOCK], m[Q_BLOCK:]),
            jnp.where(first, l[:Q_BLOCK], l[Q_BLOCK:]))


ATTN_UNROLL = 8
ATTN_STAGE_ROWS = 2048
COPY_ROWS = 512


def _attn_kernel(q_ref, k_ref, v_ref, b1_ref, b4_ref, b16_ref, o_ref,
                 stage, qd, kd, vd, m_acc, l_acc, *, seq):
    for dil, bias_ref in ((DILATIONS[2], b16_ref), (DILATIONS[1], b4_ref), (DILATIONS[0], b1_ref)):
        first_branch = dil == DILATIONS[2]
        sub_len = seq // dil
        n_blocks = sub_len // Q_BLOCK
        assert n_blocks & (n_blocks - 1) == 0 and (dil * n_blocks) % ATTN_UNROLL == 0
        w = _key_window(sub_len)

        if dil == 1:
            qs, ks, vs = q_ref, k_ref, v_ref
        else:
            seg = min(ATTN_STAGE_ROWS, seq)
            per = seg // dil

            def segment(sg, carry, dil=dil, sub_len=sub_len, seg=seg, per=per):
                base = pl.multiple_of(sg * seg, seg)
                for idx, src in enumerate((q_ref, k_ref, v_ref)):
                    for c in range(seg // COPY_ROWS):
                        stage[idx, c * COPY_ROWS:(c + 1) * COPY_ROWS, :] = (
                            src[pl.ds(base + c * COPY_ROWS, COPY_ROWS), :].astype(F32))

                def residue(r, inner):
                    dst = pl.ds(pl.multiple_of(r * sub_len + sg * per, Q_BLOCK), per)
                    for idx, dst_ref in enumerate((qd, kd, vd)):
                        dst_ref[dst, :] = stage[idx, pl.ds(r, per, stride=dil), :].astype(BF16)
                    return inner

                lax.fori_loop(0, dil, residue, 0)
                return carry

            lax.fori_loop(0, seq // seg, segment, 0)
            qs, ks, vs = qd, kd, vd

        def trip(j, carry, dil=dil, bias_ref=bias_ref, sub_len=sub_len, n_blocks=n_blocks, w=w,
                 qs=qs, ks=ks, vs=vs, first_branch=first_branch):
            for uu in range(ATTN_UNROLL):
                item = j * ATTN_UNROLL + uu
                if n_blocks == 1:
                    r, l0, ws, var = item, 0, 0, 0
                else:
                    r = jnp.right_shift(item, n_blocks.bit_length() - 1) if dil > 1 else 0
                    i = jnp.bitwise_and(item, n_blocks - 1) if dil > 1 else item
                    l0 = i * Q_BLOCK
                    ws = jnp.clip(l0 - BAND_HALF, 0, sub_len - w)
                    var = jnp.where(i == 0, 0, jnp.where(i == n_blocks - 1, 2, 1))
                base = r * sub_len
                qb = qs[pl.ds(pl.multiple_of(base + l0, Q_BLOCK), Q_BLOCK), :]
                kb = ks[pl.ds(pl.multiple_of(base + ws, BAND_HALF), w), :]
                vb = vs[pl.ds(pl.multiple_of(base + ws, BAND_HALF), w), :]
                o, m, l = _attn_tile(qb, kb, vb, bias_ref[var].reshape(HEADS_PER_BLOCK * Q_BLOCK, w))
                if dil == 1:
                    rows = pl.ds(pl.multiple_of(l0, Q_BLOCK), Q_BLOCK)
                else:
                    rows = pl.ds(r + dil * l0, Q_BLOCK, stride=dil)
                if first_branch:
                    o_ref[rows, :] = o
                    m_acc[rows, :] = m
                    l_acc[rows, :] = l
                else:
                    m_old = m_acc[rows, :]
                    m_new = jnp.maximum(m_old, m)
                    a = jnp.exp(m_old - m_new)
                    b = jnp.exp(m - m_new)
                    o_ref[rows, :] = a * o_ref[rows, :] + b * o
                    l_acc[rows, :] = a * l_acc[rows, :] + b * l
                    m_acc[rows, :] = m_new
            return carry

        lax.fori_loop(0, dil * n_blocks // ATTN_UNROLL, trip, 0)

    def finish(c, carry):
        rows = pl.ds(pl.multiple_of(c * COPY_ROWS, COPY_ROWS), COPY_ROWS)
        o_ref[rows, :] = o_ref[rows, :] / l_acc[rows, :]
        return carry

    lax.fori_loop(0, seq // COPY_ROWS, finish, 0)


def _attention(q, k, v, bias_tiles):
    b, s, a = q.shape
    n_pairs = a // LANES
    assert s % (DILATIONS[-1] * Q_BLOCK) == 0
    seq_spec = pl.BlockSpec((None, s, LANES), lambda bi, hp: (bi, 0, hp))
    bias_specs = [pl.BlockSpec((t.shape[0], HEADS_PER_BLOCK) + t.shape[2:], lambda bi, hp: (0, hp, 0, 0))
                  for t in bias_tiles]
    return pl.pallas_call(
        functools.partial(_attn_kernel, seq=s),
        grid=(b, n_pairs),
        in_specs=[seq_spec, seq_spec, seq_spec] + bias_specs,
        out_specs=seq_spec,
        out_shape=jax.ShapeDtypeStruct((b, s, a), F32),
        scratch_shapes=[pltpu.VMEM((3, min(ATTN_STAGE_ROWS, s), LANES), F32)]
        + [pltpu.VMEM((s, LANES), BF16)] * 3
        + [pltpu.VMEM((s, LANES), F32)] * 2,
        compiler_params=_cparams(("parallel", "parallel")),
        name="dilated_attn",
    )(q, k, v, *bias_tiles)


def _ssm_matrices(lam_re, lam_im, log_step, b_re, b_im, c_re, c_im):
    hp = lax.Precision.HIGHEST
    t_len, n_ch = SSM_CHUNK, SSM_GROUP
    step = jnp.exp(log_step.astype(F32))[..., None]
    lr, li = lam_re.astype(F32), lam_im.astype(F32)
    zr, zi = lr * step, li * step
    tau = jnp.arange(t_len + 1, dtype=F32)[:, None, None, None]
    mag = jnp.exp(tau * zr)
    pr, pi = mag * jnp.cos(tau * zi), mag * jnp.sin(tau * zi)
    nr, ni = pr[1] - 1.0, pi[1]
    den = lr * lr + li * li
    fr, fi = (nr * lr + ni * li) / den, (ni * lr - nr * li) / den
    bbr = fr[..., None] * b_re - fi[..., None] * b_im
    bbi = fr[..., None] * b_im + fi[..., None] * b_re
    cpr = c_re[None] * pr[:, :, :, None, :] - c_im[None] * pi[:, :, :, None, :]
    cpi = c_re[None] * pi[:, :, :, None, :] + c_im[None] * pr[:, :, :, None, :]
    kern = (jnp.einsum('tdgcn,dgnk->tdgck', cpr, bbr, precision=hp)
            - jnp.einsum('tdgcn,dgnk->tdgck', cpi, bbi, precision=hp))
    g = lam_re.shape[1]

    def toeplitz(k_dir, causal):
        taps = k_dir[:t_len]
        line = jnp.concatenate([taps, jnp.zeros_like(taps)], axis=0)
        m = jnp.tile(line, (t_len, 1, 1, 1))[:t_len * (2 * t_len - 1)]
        m = m.reshape((t_len, 2 * t_len - 1) + taps.shape[1:])[:, :t_len]
        if not causal:
            m = jnp.swapaxes(m, 0, 1)
        return jnp.transpose(m, (2, 0, 4, 1, 3)).reshape(g, CHUNK_WIDTH, CHUNK_WIDTH)

    m_sum = toeplitz(kern[:, 0], True) + toeplitz(kern[:, 1], False)

    def state_in(direction, p_r, p_i):
        p_r, p_i = p_r[:, direction], p_i[:, direction]
        re = p_r[..., None] * bbr[direction][None] - p_i[..., None] * bbi[direction][None]
        im = p_r[..., None] * bbi[direction][None] + p_i[..., None] * bbr[direction][None]
        re = jnp.transpose(re, (1, 0, 3, 2)).reshape(g, CHUNK_WIDTH, SSM_STATE)
        im = jnp.transpose(im, (1, 0, 3, 2)).reshape(g, CHUNK_WIDTH, SSM_STATE)
        return jnp.concatenate([re, im, im, re], axis=-1)

    w1 = jnp.concatenate([m_sum, state_in(0, pr[:t_len][::-1], pi[:t_len][::-1]),
                          state_in(1, pr[:t_len], pi[:t_len])], axis=-1)

    def state_out(direction, c_r, c_i):
        c_r, c_i = c_r[:, direction], c_i[:, direction]
        re = jnp.transpose(c_r, (1, 3, 0, 2)).reshape(g, SSM_STATE, CHUNK_WIDTH)
        im = jnp.transpose(-c_i, (1, 3, 0, 2)).reshape(g, SSM_STATE, CHUNK_WIDTH)
        return jnp.concatenate([re, im], axis=1)

    w2 = jnp.concatenate([state_out(0, cpr[1:], cpi[1:]),
                          state_out(1, cpr[1:][::-1], cpi[1:][::-1])], axis=1)
    ar, ai = pr[t_len], pi[t_len]
    coef = jnp.stack([jnp.concatenate([ar[0], ar[0]], -1), jnp.concatenate([-ai[0], ai[0]], -1),
                      jnp.concatenate([ar[1], ar[1]], -1), jnp.concatenate([-ai[1], ai[1]], -1)], axis=1)
    return w1.astype(BF16), w2.astype(BF16), coef


SUBLANES = 8
SCAN_PITCH_PAD = 8


def _ssm_kernel(u_ref, w1_ref, w2_ref, coef_ref, d_ref, y_ref, zf_loc, wf_loc, zb_loc, wb_loc,
                zf_in, zb_in, *, n_chunks, batch):
    rows_total = batch * n_chunks
    rc = min(512, rows_total)
    piece = min(rc, n_chunks)
    pitch = n_chunks + SCAN_PITCH_PAD
    sw = STATE_WIDTH
    octets = batch // SUBLANES

    def scratch_rows(c, p):
        first = c * rc + p * piece
        if piece == n_chunks:
            return pl.ds(pl.multiple_of((first // n_chunks) * pitch, SUBLANES), piece)
        seq = first // n_chunks
        return pl.ds(pl.multiple_of(seq * pitch + (first - seq * n_chunks), SUBLANES), piece)

    def local(c, carry):
        rows = pl.ds(pl.multiple_of(c * rc, rc), rc)
        u = u_ref[rows, :]
        res = jnp.dot(u.astype(BF16), w1_ref[...], preferred_element_type=F32)
        y_ref[rows, :] = res[:, :CHUNK_WIDTH] + u * d_ref[...]
        for p in range(rc // piece):
            srows = scratch_rows(c, p)
            for idx, dst in enumerate((zf_loc, wf_loc, zb_loc, wb_loc)):
                dst[srows, :] = res[p * piece:(p + 1) * piece,
                                    CHUNK_WIDTH + idx * sw:CHUNK_WIDTH + (idx + 1) * sw]
        return carry

    lax.fori_loop(0, rows_total // rc, local, 0)

    ar_f = jnp.broadcast_to(coef_ref[0:1, :], (SUBLANES, sw))
    ai_f = jnp.broadcast_to(coef_ref[1:2, :], (SUBLANES, sw))
    ar_b = jnp.broadcast_to(coef_ref[2:3, :], (SUBLANES, sw))
    ai_b = jnp.broadcast_to(coef_ref[3:4, :], (SUBLANES, sw))

    def step(k, state):
        new = []
        for oc in range(octets):
            zf, wf, zb, wb = state[4 * oc:4 * oc + 4]
            rf = pl.ds(oc * SUBLANES * pitch + k, SUBLANES, stride=pitch)
            rb = pl.ds(oc * SUBLANES * pitch + (n_chunks - 1 - k), SUBLANES, stride=pitch)
            zf_in[rf, :] = zf
            zb_in[rb, :] = zb
            new += [ar_f * zf + ai_f * wf + zf_loc[rf, :], ar_f * wf - ai_f * zf + wf_loc[rf, :],
                    ar_b * zb + ai_b * wb + zb_loc[rb, :], ar_b * wb - ai_b * zb + wb_loc[rb, :]]
        return tuple(new)

    zero = jnp.zeros((SUBLANES, sw), F32)
    lax.fori_loop(0, n_chunks, step, (zero,) * (4 * octets), unroll=4)

    def carried(c, carry):
        rows = pl.ds(pl.multiple_of(c * rc, rc), rc)
        pieces = [scratch_rows(c, p) for p in range(rc // piece)]
        zf = jnp.concatenate([zf_in[srows, :] for srows in pieces], axis=0)
        zb = jnp.concatenate([zb_in[srows, :] for srows in pieces], axis=0)
        y_ref[rows, :] += (jnp.dot(zf.astype(BF16), w2_ref[0:sw, :], preferred_element_type=F32)
                           + jnp.dot(zb.astype(BF16), w2_ref[sw:, :], preferred_element_type=F32))
        return carry

    lax.fori_loop(0, rows_total // rc, carried, 0)


def _ssm_scan(uc, w1, w2, coef, d_skip, batch):
    g, rows, _ = uc.shape
    n_chunks = rows // batch
    assert batch % SUBLANES == 0 and n_chunks % SUBLANES == 0
    d_rows = jnp.tile(d_skip.astype(F32).reshape(g, 1, SSM_GROUP), (1, SSM_CHUNK, 1)).reshape(g, 1, CHUNK_WIDTH)
    per_group = lambda shape: pl.BlockSpec((None,) + shape, lambda gi: (gi,) + (0,) * len(shape))
    scratch_rows = batch * (n_chunks + SCAN_PITCH_PAD)
    return pl.pallas_call(
        functools.partial(_ssm_kernel, n_chunks=n_chunks, batch=batch),
        grid=(g,),
        in_specs=[per_group((rows, CHUNK_WIDTH)), per_group(w1.shape[1:]), per_group(w2.shape[1:]),
                  per_group(coef.shape[1:]), per_group((1, CHUNK_WIDTH))],
        out_specs=per_group((rows, CHUNK_WIDTH)),
        out_shape=jax.ShapeDtypeStruct((g, rows, CHUNK_WIDTH), F32),
        scratch_shapes=[pltpu.VMEM((scratch_rows, STATE_WIDTH), F32)] * 6,
        compiler_params=_cparams(("parallel",)),
        name="s5_scan",
    )(uc, w1, w2, coef, d_rows)


def _mixout_kernel(x_ref, a_ref, yc_ref, gw_ref, gb_ref, ga_ref, gs_ref, wo_ref, o_ref, y_tok, *, aw):
    _chunks_to_tokens(yc_ref, y_tok, x_ref.shape[0] // SSM_CHUNK)
    y = jnp.concatenate([y_tok[blk] for blk in range(y_tok.shape[0])], axis=1)
    yg = 0.5 * y * (1.0 + jnp.tanh(math.sqrt(2.0 / math.pi) * (y + 0.044715 * (y * y * y))))
    gate = _sigmoid(jnp.dot(yg.astype(BF16), gw_ref[...], preferred_element_type=F32) + gb_ref[...])
    an = _rms(a_ref[...], ga_ref[...]).astype(BF16)
    sn = _rms(yg * gate, gs_ref[...]).astype(BF16)
    o_ref[...] = (x_ref[...] + jnp.dot(an, wo_ref[0:aw, :], preferred_element_type=F32)
                  + jnp.dot(sn, wo_ref[aw:, :], preferred_element_type=F32))


def _mixout(x, attn, yc, glu_w, glu_b, g_attn, g_ssm, w_out):
    t, d = x.shape
    aw = attn.shape[1]
    groups = yc.shape[0]
    sw = groups * SSM_GROUP
    tm = TOKEN_TILE
    row = lambda i: (i, 0)
    fixed = lambda i: (0, 0)
    return pl.pallas_call(
        functools.partial(_mixout_kernel, aw=aw),
        grid=(t // tm,),
        in_specs=[pl.BlockSpec((tm, d), row), pl.BlockSpec((tm, aw), row),
                  pl.BlockSpec((groups, tm // SSM_CHUNK, CHUNK_WIDTH), lambda i: (0, i, 0)),
                  pl.BlockSpec((sw, sw), fixed), pl.BlockSpec((1, sw), fixed), pl.BlockSpec((1, aw), fixed),
                  pl.BlockSpec((1, sw), fixed), pl.BlockSpec((d, d), fixed)],
        out_specs=pl.BlockSpec((tm, d), row),
        out_shape=jax.ShapeDtypeStruct((t, d), F32),
        scratch_shapes=[pltpu.VMEM((sw // LANES, tm, LANES), F32)],
        compiler_params=_cparams(("parallel",)),
        name="mixer_out",
    )(x, attn, yc, glu_w, glu_b.reshape(1, sw), g_attn.reshape(1, aw), g_ssm.reshape(1, sw), w_out)


MXU_WIDTH = 256
FF_CHUNK_TILES = 6


def _ff_chunks(d_ff):
    assert d_ff % MXU_WIDTH == 0
    tiles = d_ff // MXU_WIDTH
    n = -(-tiles // FF_CHUNK_TILES)
    sizes = [(tiles // n + (1 if c < tiles % n else 0)) * MXU_WIDTH for c in range(n)]
    return [(sum(sizes[:c]), sizes[c]) for c in range(n)]


def _swiglu(h, w1_ref, w3_ref, w2_ref):
    d_ff = w1_ref.shape[1]
    acc = None
    for start, size in _ff_chunks(d_ff):
        a = jnp.dot(h, w1_ref[:, start:start + size], preferred_element_type=F32)
        b = jnp.dot(h, w3_ref[:, start:start + size], preferred_element_type=F32)
        act = (a * _sigmoid(a) * b).astype(BF16)
        part = jnp.dot(act, w2_ref[start:start + size, :], preferred_element_type=F32)
        acc = part if acc is None else acc + part
    return acc


def _ffn_kernel(x_ref, g_ref, w1_ref, w3_ref, w2_ref, o_ref):
    x = x_ref[...]
    h = _rms(x, g_ref[...]).astype(BF16)
    o_ref[...] = x + _swiglu(h, w1_ref, w3_ref, w2_ref)


def _dense_ffn(x, g, w1, w3, w2):
    t, d = x.shape
    d_ff = w1.shape[1]
    tm = TOKEN_TILE
    row = lambda i: (i, 0)
    fixed = lambda i: (0, 0)
    resident = lambda shape: pl.BlockSpec(shape, fixed, pipeline_mode=pl.Buffered(1))
    return pl.pallas_call(
        _ffn_kernel,
        grid=(t // tm,),
        in_specs=[pl.BlockSpec((tm, d), row), pl.BlockSpec((1, d), fixed),
                  resident((d, d_ff)), resident((d, d_ff)), resident((d_ff, d))],
        out_specs=pl.BlockSpec((tm, d), row),
        out_shape=jax.ShapeDtypeStruct((t, d), F32),
        compiler_params=_cparams(("parallel",)),
        name="dense_ffn",
    )(x, g.reshape(1, d), w1, w3, w2)


def _router_kernel(x_ref, g_ref, r_ref, idx_ref, wt_ref, *, n_experts):
    h = _rms(x_ref[...], g_ref[...])
    r = r_ref[...]
    h_hi, r_hi = h.astype(BF16), r.astype(BF16)
    h_lo = (h - h_hi.astype(F32)).astype(BF16)
    r_lo = (r - r_hi.astype(F32)).astype(BF16)
    logits = (jnp.dot(h_hi, r_hi, preferred_element_type=F32) + jnp.dot(h_hi, r_lo, preferred_element_type=F32)
              + jnp.dot(h_lo, r_hi, preferred_element_type=F32))
    lane = lax.broadcasted_iota(jnp.int32, logits.shape, 1).astype(F32)
    logits = jnp.where(lane < n_experts, logits, -jnp.inf)
    v1 = jnp.max(logits, axis=1, keepdims=True)
    i1 = jnp.min(jnp.where(logits == v1, lane, float(LANES)), axis=1, keepdims=True)
    rest = jnp.where(lane == i1, -jnp.inf, logits)
    v2 = jnp.max(rest, axis=1, keepdims=True)
    i2 = jnp.min(jnp.where(rest == v2, lane, float(LANES)), axis=1, keepdims=True)
    e = jnp.exp(v2 - v1)
    w1 = 1.0 / (1.0 + e)
    w2 = e / (1.0 + e)
    idx_ref[...] = jnp.where(lane == 0.0, i1, jnp.where(lane == 1.0, i2, 0.0))[:, :8].astype(jnp.int32)
    wt_ref[...] = jnp.where(lane == 0.0, w1, jnp.where(lane == 1.0, w2, 0.0))[:, :8]


def _router(x, g, router):
    t, d = x.shape
    n_experts = router.shape[1]
    tm = TOKEN_TILE
    r_pad = jnp.zeros((d, LANES), F32).at[:, :n_experts].set(router.astype(F32))
    row = lambda i: (i, 0)
    fixed = lambda i: (0, 0)
    return pl.pallas_call(
        functools.partial(_router_kernel, n_experts=n_experts),
        grid=(t // tm,),
        in_specs=[pl.BlockSpec((tm, d), row), pl.BlockSpec((1, d), fixed), pl.BlockSpec((d, LANES), fixed)],
        out_specs=[pl.BlockSpec((tm, 8), row), pl.BlockSpec((tm, 8), row)],
        out_shape=[jax.ShapeDtypeStruct((t, 8), jnp.int32), jax.ShapeDtypeStruct((t, 8), F32)],
        compiler_params=_cparams(("parallel",)),
        name="router",
    )(x, g.reshape(1, d), r_pad)


def _row_copy(src, src_row, dst, dst_row, sem):
    return pltpu.make_async_copy(src.at[pl.ds(src_row, 1), :], dst.at[pl.ds(dst_row, 1), :], sem)


MOE_PARTS = 4


def _moe_kernel(te_ref, src_ref, srcn_ref, dst_ref, x_hbm, g_ref, w1_ref, w3_ref, w2_ref, o_hbm,
                xb0, xb1, xb2, xb3, yb0, yb1, yb2, yb3, gsem, ssem, *, tm):
    del te_ref
    i = pl.program_id(0)
    last = pl.num_programs(0) - 1
    rows = tm // MOE_PARTS
    xbufs, ybufs = (xb0, xb1, xb2, xb3), (yb0, yb1, yb2, yb3)

    def start_gather(idx_ref, part, s):
        for j in range(rows):
            _row_copy(x_hbm, idx_ref[0, part * rows + j], xbufs[s], j, gsem.at[s]).start()

    def wait_gather(s):
        for j in range(rows):
            _row_copy(x_hbm, 0, xbufs[s], j, gsem.at[s]).wait()

    def start_scatter(s):
        for j in range(rows):
            _row_copy(ybufs[s], j, o_hbm, dst_ref[0, s * rows + j], ssem.at[s]).start()

    def wait_scatter(s):
        for j in range(rows):
            _row_copy(ybufs[s], j, o_hbm, 0, ssem.at[s]).wait()

    def compute(s):
        h = _rms(xbufs[s][...], g_ref[...]).astype(BF16)
        ybufs[s][...] = _swiglu(h, w1_ref, w3_ref, w2_ref)

    @pl.when(i == 0)
    def _():
        start_gather(src_ref, 0, 0)
        wait_gather(0)

    @pl.when(i > 0)
    def _():
        wait_scatter(2)
        wait_scatter(3)

    start_gather(src_ref, 1, 1)
    compute(0)
    wait_gather(1)
    start_gather(src_ref, 2, 2)
    start_scatter(0)
    compute(1)
    wait_gather(2)
    start_gather(src_ref, 3, 3)
    start_scatter(1)
    start_gather(srcn_ref, 0, 0)
    compute(2)
    wait_gather(3)
    wait_gather(0)
    wait_scatter(0)
    wait_scatter(1)
    start_scatter(2)
    compute(3)
    start_scatter(3)

    @pl.when(i == last)
    def _():
        wait_scatter(2)
        wait_scatter(3)


def _moe_plan(idx, n_experts, tm):
    i32 = jnp.int32
    t = idx.shape[0]
    n_assign = TOP_K * t
    pad = n_experts * tm
    n_rows = n_assign + pad
    n_tiles = n_rows // tm
    flat = idx.reshape(-1)
    experts = jnp.arange(n_experts, dtype=i32)
    keys = jnp.sort(flat * n_assign + jnp.arange(n_assign, dtype=i32))
    counts = jnp.sum((flat[:, None] == experts[None, :]).astype(i32), axis=0)
    padded = (counts + tm - 1) // tm * tm
    ends = jnp.cumsum(padded)
    off = jnp.cumsum(counts) - counts
    poff = ends - padded
    rows = jnp.arange(n_rows, dtype=i32)
    e_row = jnp.minimum(jnp.sum((rows[:, None] >= ends[None, :]).astype(i32), axis=1), n_experts - 1)
    onehot = (e_row[:, None] == experts[None, :]).astype(i32)
    rel = rows - jnp.sum(onehot * poff[None, :], axis=1)
    count_row = jnp.sum(onehot * counts[None, :], axis=1)
    valid = rel < count_row
    keys_padded = jnp.concatenate([keys, jnp.zeros((pad,), i32)])
    key_row = jnp.zeros((n_rows,), i32)
    for e in range(n_experts):
        key_row = jnp.where(e_row == e, jnp.roll(keys_padded, poff[e] - off[e]), key_row)
    assign = key_row - e_row * n_assign
    token, slot = assign // TOP_K, assign % TOP_K
    valid_before = jnp.sum(onehot * off[None, :], axis=1) + jnp.minimum(rel, count_row)
    src = jnp.where(valid, token, 0)
    dst = jnp.where(valid, slot * t + token, n_assign + rows - valid_before)
    tile_e = e_row.reshape(n_tiles, tm)[:, 0]
    return src.reshape(n_tiles, 1, tm), dst.reshape(n_tiles, 1, tm), tile_e


def _moe_experts(x, g, w1, w3, w2, src, dst, tile_e):
    t, d = x.shape
    d_ff = w1.shape[2]
    n_tiles, _, tm = src.shape
    smem_row = lambda fn: pl.BlockSpec((None, 1, tm), fn, memory_space=pltpu.SMEM)
    expert = lambda shape: pl.BlockSpec((None,) + shape, lambda i, te: (te[i], 0, 0))
    grid_spec = pltpu.PrefetchScalarGridSpec(
        num_scalar_prefetch=1,
        grid=(n_tiles,),
        in_specs=[smem_row(lambda i, te: (i, 0, 0)),
                  smem_row(lambda i, te: (jnp.minimum(i + 1, n_tiles - 1), 0, 0)),
                  smem_row(lambda i, te: (i, 0, 0)),
                  pl.BlockSpec(memory_space=pl.ANY),
                  pl.BlockSpec((1, d), lambda i, te: (0, 0)),
                  expert((d, d_ff)), expert((d, d_ff)), expert((d_ff, d))],
        out_specs=pl.BlockSpec(memory_space=pl.ANY),
        scratch_shapes=[pltpu.VMEM((tm // MOE_PARTS, d), F32)] * (2 * MOE_PARTS)
        + [pltpu.SemaphoreType.DMA((MOE_PARTS,)), pltpu.SemaphoreType.DMA((MOE_PARTS,))],
    )
    return pl.pallas_call(
        functools.partial(_moe_kernel, tm=tm),
        grid_spec=grid_spec,
        out_shape=jax.ShapeDtypeStruct((n_tiles * tm, d), F32),
        compiler_params=_cparams(("arbitrary",)),
        name="moe_experts",
    )(tile_e, src, src, dst, x, g.reshape(1, d), w1, w3, w2)


def _combine_kernel(x_ref, y0_ref, y1_ref, wt_ref, g_ref, o_ref, *, final_norm):
    wt = wt_ref[...]
    out = x_ref[...] + wt[:, 0:1] * y0_ref[...] + wt[:, 1:2] * y1_ref[...]
    if final_norm:
        out = _rms(out, g_ref[...])
    o_ref[...] = out


def _moe_combine(x, y2, wt, g_final, final_norm):
    t, d = x.shape
    tm = TOKEN_TILE
    row = lambda i: (i, 0)
    return pl.pallas_call(
        functools.partial(_combine_kernel, final_norm=final_norm),
        grid=(t // tm,),
        in_specs=[pl.BlockSpec((tm, d), row), pl.BlockSpec((tm, d), row),
                  pl.BlockSpec((tm, d), lambda i: (i + t // tm, 0)),
                  pl.BlockSpec((tm, 8), row), pl.BlockSpec((1, d), lambda i: (0, 0))],
        out_specs=pl.BlockSpec((tm, d), row),
        out_shape=jax.ShapeDtypeStruct((t, d), F32),
        compiler_params=_cparams(("parallel",)),
        name="moe_combine",
    )(x, y2, y2, wt, g_final.reshape(1, d))


def _norm_kernel(x_ref, g_ref, o_ref):
    o_ref[...] = _rms(x_ref[...], g_ref[...])


def _final_norm(x, g):
    t, d = x.shape
    tm = TOKEN_TILE
    return pl.pallas_call(
        _norm_kernel,
        grid=(t // tm,),
        in_specs=[pl.BlockSpec((tm, d), lambda i: (i, 0)), pl.BlockSpec((1, d), lambda i: (0, 0))],
        out_specs=pl.BlockSpec((tm, d), lambda i: (i, 0)),
        out_shape=jax.ShapeDtypeStruct((t, d), F32),
        compiler_params=_cparams(("parallel",)),
        name="final_norm",
    )(x, g.reshape(1, d))


def _trunk(x, p):
    b, s, d = x.shape
    t = b * s
    aw = p['attn_out_norm'].shape[1]
    depth = p['w_in'].shape[0]
    xt = x.reshape(t, d)
    normed = False
    for layer in range(depth):
        q, k, v, uc = _inproj(xt, p['norm_mix'][layer], p['w_in'][layer], aw)
        shape3 = lambda z: z.reshape(b, s, z.shape[1])
        attn = _attention(shape3(q), shape3(k), shape3(v), p['bias_tiles'][s]).reshape(t, aw)
        w1m, w2m, coef = p['ssm'][layer]
        yc = _ssm_scan(uc, w1m, w2m, coef, p['ssm_d'][layer], b)
        xt = _mixout(xt, attn, yc, p['glu_w'][layer], p['glu_b'][layer], p['attn_out_norm'][layer],
                     p['ssm_out_norm'][layer], p['w_out'][layer])
        g = p['norm_ffn'][layer]
        j = layer // 2
        if layer % 2 == 0:
            xt = _dense_ffn(xt, g, p['ffn_w1'][j], p['ffn_w3'][j], p['ffn_w2'][j])
        else:
            idx, wt = _router(xt, g, p['moe_router'][j])
            n_experts = p['moe_router'].shape[2]
            src, dst, tile_e = _moe_plan(idx[:, :TOP_K], n_experts, MOE_TILE)
            y2 = _moe_experts(xt, g, p['moe_w1'][j], p['moe_w3'][j], p['moe_w2'][j], src, dst, tile_e)
            last = layer == depth - 1
            xt = _moe_combine(xt, y2, wt, p['final_norm'], last)
            normed = last
    if not normed:
        xt = _final_norm(xt, p['final_norm'])
    return xt.reshape(b, s, d)


def kernel(x_prompt, x_sample, rel_bias, norm_mix, w_in, ssm_lambda_re, ssm_lambda_im, ssm_log_step, ssm_b_re, ssm_b_im, ssm_c_re, ssm_c_im, ssm_d, glu_w, glu_b, attn_out_norm, ssm_out_norm, w_out, norm_ffn, ffn_w1, ffn_w3, ffn_w2, moe_router, moe_w1, moe_w3, moe_w2, final_norm):
    depth = w_in.shape[0]
    bias_tiles = {}
    for x in (x_prompt, x_sample):
        s = x.shape[1]
        if s not in bias_tiles:
            bias_tiles[s] = [_bias_tiles(rel_bias, dil, s // dil) for dil in DILATIONS]
    p = {
        'bias_tiles': bias_tiles,
        'norm_mix': norm_mix, 'w_in': w_in.astype(BF16),
        'ssm': [_ssm_matrices(ssm_lambda_re[l], ssm_lambda_im[l], ssm_log_step[l], ssm_b_re[l], ssm_b_im[l],
                              ssm_c_re[l], ssm_c_im[l]) for l in range(depth)],
        'ssm_d': ssm_d, 'glu_w': glu_w.astype(BF16), 'glu_b': glu_b,
        'attn_out_norm': attn_out_norm, 'ssm_out_norm': ssm_out_norm, 'w_out': w_out.astype(BF16),
        'norm_ffn': norm_ffn,
        'ffn_w1': ffn_w1.astype(BF16), 'ffn_w3': ffn_w3.astype(BF16), 'ffn_w2': ffn_w2.astype(BF16),
        'moe_router': moe_router,
        'moe_w1': moe_w1.astype(BF16), 'moe_w3': moe_w3.astype(BF16), 'moe_w2': moe_w2.astype(BF16),
        'final_norm': final_norm,
    }
    return (_trunk(x_prompt, p), _trunk(x_sample, p))
```

```python
import functools
import math

import numpy as np
import jax
import jax.numpy as jnp
from jax import lax
from jax.experimental import pallas as pl
from jax.experimental.pallas import tpu as pltpu

F32 = jnp.float32
BF16 = jnp.bfloat16

LANES = 128
VMEM_LIMIT_BYTES = 56 * 1024 * 1024

HEAD_DIM = 64
HEADS_PER_BLOCK = LANES // HEAD_DIM
DILATIONS = (1, 4, 16)
BAND_HALF = 64
Q_BLOCK = 128
N_REL_BUCKETS = 32
REL_MAX_DISTANCE = 1024
SSM_GROUP = 16
SSM_STATE = 64
GROUPS_PER_BLOCK = LANES // SSM_GROUP
SSM_CHUNK = 16
CHUNK_WIDTH = SSM_CHUNK * SSM_GROUP
STATE_WIDTH = 2 * SSM_STATE
TOP_K = 2
NORM_EPS = 1e-6
MASK_VALUE = -1e30

TOKEN_TILE = 512
MOE_TILE = 1024


def _cparams(semantics):
    return pltpu.CompilerParams(dimension_semantics=semantics, vmem_limit_bytes=VMEM_LIMIT_BYTES)


def _rms(x, g):
    return x * lax.rsqrt(jnp.mean(x * x, axis=-1, keepdims=True) + NORM_EPS) * g


def _sigmoid(x):
    return 1.0 / (1.0 + jnp.exp(-x))


def _granule_transpose(vs):
    lane_granule = lax.broadcasted_iota(jnp.int32, vs[0].shape, 1) // SSM_GROUP
    for dist in (4, 2, 1):
        upper = (lane_granule & dist) != 0
        new = list(vs)
        for i in range(GROUPS_PER_BLOCK):
            if i & dist == 0:
                j = i + dist
                new[i] = jnp.where(upper, pltpu.roll(vs[j], SSM_GROUP * dist, 1), vs[i])
                new[j] = jnp.where(upper, vs[j], pltpu.roll(vs[i], LANES - SSM_GROUP * dist, 1))
        vs = new
    return vs


def _tokens_to_chunks(tok_ref, chunk_ref, n_chunks):
    for blk in range(tok_ref.shape[0]):
        for half in range(CHUNK_WIDTH // LANES):
            by_token = [tok_ref[blk, pl.ds(half * GROUPS_PER_BLOCK + t, n_chunks, stride=SSM_CHUNK), :]
                        for t in range(GROUPS_PER_BLOCK)]
            for g, val in enumerate(_granule_transpose(by_token)):
                chunk_ref[blk * GROUPS_PER_BLOCK + g, :, half * LANES:(half + 1) * LANES] = val


def _chunks_to_tokens(chunk_ref, tok_ref, n_chunks):
    for blk in range(tok_ref.shape[0]):
        for half in range(CHUNK_WIDTH // LANES):
            by_group = [chunk_ref[blk * GROUPS_PER_BLOCK + g, :, half * LANES:(half + 1) * LANES]
                        for g in range(GROUPS_PER_BLOCK)]
            for t, val in enumerate(_granule_transpose(by_group)):
                tok_ref[blk, pl.ds(half * GROUPS_PER_BLOCK + t, n_chunks, stride=SSM_CHUNK), :] = val


def _inproj_kernel(x_ref, g_ref, w_ref, q_ref, k_ref, v_ref, uc_ref, u_tok, *, aw):
    h = _rms(x_ref[...], g_ref[...]).astype(BF16)
    scale = HEAD_DIM ** -0.5
    q_ref[...] = (jnp.dot(h, w_ref[:, 0:aw], preferred_element_type=F32) * scale).astype(BF16)
    k_ref[...] = jnp.dot(h, w_ref[:, aw:2 * aw], preferred_element_type=F32).astype(BF16)
    v_ref[...] = jnp.dot(h, w_ref[:, 2 * aw:3 * aw], preferred_element_type=F32).astype(BF16)
    u = jnp.dot(h, w_ref[:, 3 * aw:], preferred_element_type=F32)
    for blk in range(u_tok.shape[0]):
        u_tok[blk] = u[:, blk * LANES:(blk + 1) * LANES]
    _tokens_to_chunks(u_tok, uc_ref, x_ref.shape[0] // SSM_CHUNK)


def _inproj(x, g, w_bf16, aw):
    t, d = x.shape
    n = w_bf16.shape[1]
    sw = n - 3 * aw
    tm = TOKEN_TILE
    groups, tile_chunks = sw // SSM_GROUP, tm // SSM_CHUNK
    row = lambda i: (i, 0)
    fixed = lambda i: (0, 0)
    return pl.pallas_call(
        functools.partial(_inproj_kernel, aw=aw),
        grid=(t // tm,),
        in_specs=[pl.BlockSpec((tm, d), row), pl.BlockSpec((1, d), fixed), pl.BlockSpec((d, n), fixed)],
        out_specs=[pl.BlockSpec((tm, aw), row), pl.BlockSpec((tm, aw), row), pl.BlockSpec((tm, aw), row),
                   pl.BlockSpec((groups, tile_chunks, CHUNK_WIDTH), lambda i: (0, i, 0))],
        out_shape=[jax.ShapeDtypeStruct((t, aw), BF16)] * 3
        + [jax.ShapeDtypeStruct((groups, t // SSM_CHUNK, CHUNK_WIDTH), F32)],
        scratch_shapes=[pltpu.VMEM((sw // LANES, tm, LANES), F32)],
        compiler_params=_cparams(("parallel",)),
        name="inproj",
    )(x, g.reshape(1, d), w_bf16)


def _t5_bucket(rel):
    nb = N_REL_BUCKETS // 2
    max_exact = nb // 2
    ret = (rel > 0).astype(np.int32) * nb
    n = np.abs(rel)
    large = max_exact + (np.log(np.maximum(n, 1) / max_exact)
                         / np.log(REL_MAX_DISTANCE / max_exact) * (nb - max_exact)).astype(np.int32)
    large = np.minimum(large, nb - 1)
    return (ret + np.where(n < max_exact, n, large)).astype(np.int32)


def _key_window(sub_len):
    return min(2 * Q_BLOCK, sub_len)


def _bias_tiles(rel_bias, dilation, sub_len):
    w = _key_window(sub_len)
    offsets = (0, -BAND_HALF, -2 * BAND_HALF) if sub_len > Q_BLOCK else (0,)
    heads = rel_bias.shape[1]
    span = Q_BLOCK + w
    k = np.arange(span)
    tiles = []
    for off in offsets:
        rel = np.where(k < w, k, k - span) + off
        line = rel_bias[jnp.asarray(_t5_bucket(rel * dilation))].astype(F32)
        line = jnp.where(jnp.asarray(np.abs(rel) <= BAND_HALF)[:, None], line, MASK_VALUE)
        rows = jnp.tile(line, (Q_BLOCK, 1))[:Q_BLOCK * (span - 1)].reshape(Q_BLOCK, span - 1, heads)
        tiles.append(jnp.transpose(rows[:, :w], (2, 0, 1)))
    return jnp.stack(tiles)


def _attn_tile(qb, kb, vb, bias):
    lane = lax.broadcasted_iota(jnp.int32, qb.shape, 1)
    zero = jnp.zeros_like(qb)
    q2 = jnp.concatenate([jnp.where(lane < HEAD_DIM, qb, zero), jnp.where(lane >= HEAD_DIM, qb, zero)], axis=0)
    s = lax.dot_general(q2, kb, (((1,), (1,)), ((), ())), preferred_element_type=F32) + bias
    m = jnp.max(s, axis=1, keepdims=True)
    p = jnp.exp(s - m)
    l = jnp.sum(p, axis=1, keepdims=True)
    o = jnp.dot(p.astype(BF16), vb, preferred_element_type=F32)
    first = lax.broadcasted_iota(jnp.int32, (Q_BLOCK, LANES), 1) < HEAD_DIM
    return (jnp.where(first, o[:Q_BLOCK], o[Q_BLOCK:]), jnp.where(first, m[:Q_BLOCK], m[Q_BLOCK:]),
            jnp.where(first, l[:Q_BLOCK], l[Q_BLOCK:]))


ATTN_UNROLL = 8
ATTN_STAGE_ROWS = 2048
COPY_ROWS = 512


def _attn_kernel(q_ref, k_ref, v_ref, b1_ref, b4_ref, b16_ref, o_ref,
                 stage, qd, kd, vd, m_acc, l_acc, *, seq):
    for dil, bias_ref in ((DILATIONS[2], b16_ref), (DILATIONS[1], b4_ref), (DILATIONS[0], b1_ref)):
        first_branch = dil == DILATIONS[2]
        sub_len = seq // dil
        n_blocks = sub_len // Q_BLOCK
        assert n_blocks & (n_blocks - 1) == 0 and (dil * n_blocks) % ATTN_UNROLL == 0
        w = _key_window(sub_len)

        if dil == 1:
            qs, ks, vs = q_ref, k_ref, v_ref
        else:
            seg = min(ATTN_STAGE_ROWS, seq)
            per = seg // dil

            def segment(sg, carry, dil=dil, sub_len=sub_len, seg=seg, per=per):
                base = pl.multiple_of(sg * seg, seg)
                for idx, src in enumerate((q_ref, k_ref, v_ref)):
                    for c in range(seg // COPY_ROWS):
                        stage[idx, c * COPY_ROWS:(c + 1) * COPY_ROWS, :] = (
                            src[pl.ds(base + c * COPY_ROWS, COPY_ROWS), :].astype(F32))

                def residue(r, inner):
                    dst = pl.ds(pl.multiple_of(r * sub_len + sg * per, Q_BLOCK), per)
                    for idx, dst_ref in enumerate((qd, kd, vd)):
                        dst_ref[dst, :] = stage[idx, pl.ds(r, per, stride=dil), :].astype(BF16)
                    return inner

                lax.fori_loop(0, dil, residue, 0)
                return carry

            lax.fori_loop(0, seq // seg, segment, 0)
            qs, ks, vs = qd, kd, vd

        def trip(j, carry, dil=dil, bias_ref=bias_ref, sub_len=sub_len, n_blocks=n_blocks, w=w,
                 qs=qs, ks=ks, vs=vs, first_branch=first_branch):
            for uu in range(ATTN_UNROLL):
                item = j * ATTN_UNROLL + uu
                if n_blocks == 1:
                    r, l0, ws, var = item, 0, 0, 0
                else:
                    r = jnp.right_shift(item, n_blocks.bit_length() - 1) if dil > 1 else 0
                    i = jnp.bitwise_and(item, n_blocks - 1) if dil > 1 else item
                    l0 = i * Q_BLOCK
                    ws = jnp.clip(l0 - BAND_HALF, 0, sub_len - w)
                    var = jnp.where(i == 0, 0, jnp.where(i == n_blocks - 1, 2, 1))
                base = r * sub_len
                qb = qs[pl.ds(pl.multiple_of(base + l0, Q_BLOCK), Q_BLOCK), :]
                kb = ks[pl.ds(pl.multiple_of(base + ws, BAND_HALF), w), :]
                vb = vs[pl.ds(pl.multiple_of(base + ws, BAND_HALF), w), :]
                o, m, l = _attn_tile(qb, kb, vb, bias_ref[var].reshape(HEADS_PER_BLOCK * Q_BLOCK, w))
                if dil == 1:
                    rows = pl.ds(pl.multiple_of(l0, Q_BLOCK), Q_BLOCK)
                else:
                    rows = pl.ds(r + dil * l0, Q_BLOCK, stride=dil)
                if first_branch:
                    o_ref[rows, :] = o
                    m_acc[rows, :] = m
                    l_acc[rows, :] = l
                else:
                    m_old = m_acc[rows, :]
                    m_new = jnp.maximum(m_old, m)
                    a = jnp.exp(m_old - m_new)
                    b = jnp.exp(m - m_new)
                    o_ref[rows, :] = a * o_ref[rows, :] + b * o
                    l_acc[rows, :] = a * l_acc[rows, :] + b * l
                    m_acc[rows, :] = m_new
            return carry

        lax.fori_loop(0, dil * n_blocks // ATTN_UNROLL, trip, 0)

    def finish(c, carry):
        rows = pl.ds(pl.multiple_of(c * COPY_ROWS, COPY_ROWS), COPY_ROWS)
        o_ref[rows, :] = o_ref[rows, :] / l_acc[rows, :]
        return carry

    lax.fori_loop(0, seq // COPY_ROWS, finish, 0)


def _attention(q, k, v, bias_tiles):
    b, s, a = q.shape
    n_pairs = a // LANES
    assert s % (DILATIONS[-1] * Q_BLOCK) == 0
    seq_spec = pl.BlockSpec((None, s, LANES), lambda bi, hp: (bi, 0, hp))
    bias_specs = [pl.BlockSpec((t.shape[0], HEADS_PER_BLOCK) + t.shape[2:], lambda bi, hp: (0, hp, 0, 0))
                  for t in bias_tiles]
    return pl.pallas_call(
        functools.partial(_attn_kernel, seq=s),
        grid=(b, n_pairs),
        in_specs=[seq_spec, seq_spec, seq_spec] + bias_specs,
        out_specs=seq_spec,
        out_shape=jax.ShapeDtypeStruct((b, s, a), F32),
        scratch_shapes=[pltpu.VMEM((3, min(ATTN_STAGE_ROWS, s), LANES), F32)]
        + [pltpu.VMEM((s, LANES), BF16)] * 3
        + [pltpu.VMEM((s, LANES), F32)] * 2,
        compiler_params=_cparams(("parallel", "parallel")),
        name="dilated_attn",
    )(q, k, v, *bias_tiles)


def _ssm_matrices(lam_re, lam_im, log_step, b_re, b_im, c_re, c_im):
    hp = lax.Precision.HIGHEST
    t_len, n_ch = SSM_CHUNK, SSM_GROUP
    step = jnp.exp(log_step.astype(F32))[..., None]
    lr, li = lam_re.astype(F32), lam_im.astype(F32)
    zr, zi = lr * step, li * step
    tau = jnp.arange(t_len + 1, dtype=F32)[:, None, None, None]
    mag = jnp.exp(tau * zr)
    pr, pi = mag * jnp.cos(tau * zi), mag * jnp.sin(tau * zi)
    nr, ni = pr[1] - 1.0, pi[1]
    den = lr * lr + li * li
    fr, fi = (nr * lr + ni * li) / den, (ni * lr - nr * li) / den
    bbr = fr[..., None] * b_re - fi[..., None] * b_im
    bbi = fr[..., None] * b_im + fi[..., None] * b_re
    cpr = c_re[None] * pr[:, :, :, None, :] - c_im[None] * pi[:, :, :, None, :]
    cpi = c_re[None] * pi[:, :, :, None, :] + c_im[None] * pr[:, :, :, None, :]
    kern = (jnp.einsum('tdgcn,dgnk->tdgck', cpr, bbr, precision=hp)
            - jnp.einsum('tdgcn,dgnk->tdgck', cpi, bbi, precision=hp))
    g = lam_re.shape[1]

    def toeplitz(k_dir, causal):
        taps = k_dir[:t_len]
        line = jnp.concatenate([taps, jnp.zeros_like(taps)], axis=0)
        m = jnp.tile(line, (t_len, 1, 1, 1))[:t_len * (2 * t_len - 1)]
        m = m.reshape((t_len, 2 * t_len - 1) + taps.shape[1:])[:, :t_len]
        if not causal:
            m = jnp.swapaxes(m, 0, 1)
        return jnp.transpose(m, (2, 0, 4, 1, 3)).reshape(g, CHUNK_WIDTH, CHUNK_WIDTH)

    m_sum = toeplitz(kern[:, 0], True) + toeplitz(kern[:, 1], False)

    def state_in(direction, p_r, p_i):
        p_r, p_i = p_r[:, direction], p_i[:, direction]
        re = p_r[..., None] * bbr[direction][None] - p_i[..., None] * bbi[direction][None]
        im = p_r[..., None] * bbi[direction][None] + p_i[..., None] * bbr[direction][None]
        re = jnp.transpose(re, (1, 0, 3, 2)).reshape(g, CHUNK_WIDTH, SSM_STATE)
        im = jnp.transpose(im, (1, 0, 3, 2)).reshape(g, CHUNK_WIDTH, SSM_STATE)
        return jnp.concatenate([re, im, im, re], axis=-1)

    w1 = jnp.concatenate([m_sum, state_in(0, pr[:t_len][::-1], pi[:t_len][::-1]),
                          state_in(1, pr[:t_len], pi[:t_len])], axis=-1)

    def state_out(direction, c_r, c_i):
        c_r, c_i = c_r[:, direction], c_i[:, direction]
        re = jnp.transpose(c_r, (1, 3, 0, 2)).reshape(g, SSM_STATE, CHUNK_WIDTH)
        im = jnp.transpose(-c_i, (1, 3, 0, 2)).reshape(g, SSM_STATE, CHUNK_WIDTH)
        return jnp.concatenate([re, im], axis=1)

    w2 = jnp.concatenate([state_out(0, cpr[1:], cpi[1:]),
                          state_out(1, cpr[1:][::-1], cpi[1:][::-1])], axis=1)
    ar, ai = pr[t_len], pi[t_len]
    coef = jnp.stack([jnp.concatenate([ar[0], ar[0]], -1), jnp.concatenate([-ai[0], ai[0]], -1),
                      jnp.concatenate([ar[1], ar[1]], -1), jnp.concatenate([-ai[1], ai[1]], -1)], axis=1)
    return w1.astype(BF16), w2.astype(BF16), coef


SUBLANES = 8
SCAN_PITCH_PAD = 8


def _ssm_kernel(u_ref, w1_ref, w2_ref, coef_ref, d_ref, y_ref, zf_loc, wf_loc, zb_loc, wb_loc,
                zf_in, zb_in, *, n_chunks, batch):
    rows_total = batch * n_chunks
    rc = min(512, rows_total)
    piece = min(rc, n_chunks)
    pitch = n_chunks + SCAN_PITCH_PAD
    sw = STATE_WIDTH
    octets = batch // SUBLANES

    def scratch_rows(c, p):
        first = c * rc + p * piece
        if piece == n_chunks:
            return pl.ds(pl.multiple_of((first // n_chunks) * pitch, SUBLANES), piece)
        seq = first // n_chunks
        return pl.ds(pl.multiple_of(seq * pitch + (first - seq * n_chunks), SUBLANES), piece)

    def local(c, carry):
        rows = pl.ds(pl.multiple_of(c * rc, rc), rc)
        u = u_ref[rows, :]
        res = jnp.dot(u.astype(BF16), w1_ref[...], preferred_element_type=F32)
        y_ref[rows, :] = res[:, :CHUNK_WIDTH] + u * d_ref[...]
        for p in range(rc // piece):
            srows = scratch_rows(c, p)
            for idx, dst in enumerate((zf_loc, wf_loc, zb_loc, wb_loc)):
                dst[srows, :] = res[p * piece:(p + 1) * piece,
                                    CHUNK_WIDTH + idx * sw:CHUNK_WIDTH + (idx + 1) * sw]
        return carry

    lax.fori_loop(0, rows_total // rc, local, 0)

    ar_f = jnp.broadcast_to(coef_ref[0:1, :], (SUBLANES, sw))
    ai_f = jnp.broadcast_to(coef_ref[1:2, :], (SUBLANES, sw))
    ar_b = jnp.broadcast_to(coef_ref[2:3, :], (SUBLANES, sw))
    ai_b = jnp.broadcast_to(coef_ref[3:4, :], (SUBLANES, sw))

    def step(k, state):
        new = []
        for oc in range(octets):
            zf, wf, zb, wb = state[4 * oc:4 * oc + 4]
            rf = pl.ds(oc * SUBLANES * pitch + k, SUBLANES, stride=pitch)
            rb = pl.ds(oc * SUBLANES * pitch + (n_chunks - 1 - k), SUBLANES, stride=pitch)
            zf_in[rf, :] = zf
            zb_in[rb, :] = zb
            new += [ar_f * zf + ai_f * wf + zf_loc[rf, :], ar_f * wf - ai_f * zf + wf_loc[rf, :],
                    ar_b * zb + ai_b * wb + zb_loc[rb, :], ar_b * wb - ai_b * zb + wb_loc[rb, :]]
        return tuple(new)

    zero = jnp.zeros((SUBLANES, sw), F32)
    lax.fori_loop(0, n_chunks, step, (zero,) * (4 * octets), unroll=4)

    def carried(c, carry):
        rows = pl.ds(pl.multiple_of(c * rc, rc), rc)
        pieces = [scratch_rows(c, p) for p in range(rc // piece)]
        zf = jnp.concatenate([zf_in[srows, :] for srows in pieces], axis=0)
        zb = jnp.concatenate([zb_in[srows, :] for srows in pieces], axis=0)
        y_ref[rows, :] += (jnp.dot(zf.astype(BF16), w2_ref[0:sw, :], preferred_element_type=F32)
                           + jnp.dot(zb.astype(BF16), w2_ref[sw:, :], preferred_element_type=F32))
        return carry

    lax.fori_loop(0, rows_total // rc, carried, 0)


def _ssm_scan(uc, w1, w2, coef, d_skip, batch):
    g, rows, _ = uc.shape
    n_chunks = rows // batch
    assert batch % SUBLANES == 0 and n_chunks % SUBLANES == 0
    d_rows = jnp.tile(d_skip.astype(F32).reshape(g, 1, SSM_GROUP), (1, SSM_CHUNK, 1)).reshape(g, 1, CHUNK_WIDTH)
    per_group = lambda shape: pl.BlockSpec((None,) + shape, lambda gi: (gi,) + (0,) * len(shape))
    scratch_rows = batch * (n_chunks + SCAN_PITCH_PAD)
    return pl.pallas_call(
        functools.partial(_ssm_kernel, n_chunks=n_chunks, batch=batch),
        grid=(g,),
        in_specs=[per_group((rows, CHUNK_WIDTH)), per_group(w1.shape[1:]), per_group(w2.shape[1:]),
                  per_group(coef.shape[1:]), per_group((1, CHUNK_WIDTH))],
        out_specs=per_group((rows, CHUNK_WIDTH)),
        out_shape=jax.ShapeDtypeStruct((g, rows, CHUNK_WIDTH), F32),
        scratch_shapes=[pltpu.VMEM((scratch_rows, STATE_WIDTH), F32)] * 6,
        compiler_params=_cparams(("parallel",)),
        name="s5_scan",
    )(uc, w1, w2, coef, d_rows)


def _mixout_kernel(x_ref, a_ref, yc_ref, gw_ref, gb_ref, ga_ref, gs_ref, wo_ref, o_ref, y_tok, *, aw):
    _chunks_to_tokens(yc_ref, y_tok, x_ref.shape[0] // SSM_CHUNK)
    y = jnp.concatenate([y_tok[blk] for blk in range(y_tok.shape[0])], axis=1)
    yg = 0.5 * y * (1.0 + jnp.tanh(math.sqrt(2.0 / math.pi) * (y + 0.044715 * (y * y * y))))
    gate = _sigmoid(jnp.dot(yg.astype(BF16), gw_ref[...], preferred_element_type=F32) + gb_ref[...])
    an = _rms(a_ref[...], ga_ref[...]).astype(BF16)
    sn = _rms(yg * gate, gs_ref[...]).astype(BF16)
    o_ref[...] = (x_ref[...] + jnp.dot(an, wo_ref[0:aw, :], preferred_element_type=F32)
                  + jnp.dot(sn, wo_ref[aw:, :], preferred_element_type=F32))


def _mixout(x, attn, yc, glu_w, glu_b, g_attn, g_ssm, w_out):
    t, d = x.shape
    aw = attn.shape[1]
    groups = yc.shape[0]
    sw = groups * SSM_GROUP
    tm = TOKEN_TILE
    row = lambda i: (i, 0)
    fixed = lambda i: (0, 0)
    return pl.pallas_call(
        functools.partial(_mixout_kernel, aw=aw),
        grid=(t // tm,),
        in_specs=[pl.BlockSpec((tm, d), row), pl.BlockSpec((tm, aw), row),
                  pl.BlockSpec((groups, tm // SSM_CHUNK, CHUNK_WIDTH), lambda i: (0, i, 0)),
                  pl.BlockSpec((sw, sw), fixed), pl.BlockSpec((1, sw), fixed), pl.BlockSpec((1, aw), fixed),
                  pl.BlockSpec((1, sw), fixed), pl.BlockSpec((d, d), fixed)],
        out_specs=pl.BlockSpec((tm, d), row),
        out_shape=jax.ShapeDtypeStruct((t, d), F32),
        scratch_shapes=[pltpu.VMEM((sw // LANES, tm, LANES), F32)],
        compiler_params=_cparams(("parallel",)),
        name="mixer_out",
    )(x, attn, yc, glu_w, glu_b.reshape(1, sw), g_attn.reshape(1, aw), g_ssm.reshape(1, sw), w_out)


MXU_WIDTH = 256
FF_CHUNK_TILES = 6


def _ff_chunks(d_ff):
    assert d_ff % MXU_WIDTH == 0
    tiles = d_ff // MXU_WIDTH
    n = -(-tiles // FF_CHUNK_TILES)
    sizes = [(tiles // n + (1 if c < tiles % n else 0)) * MXU_WIDTH for c in range(n)]
    return [(sum(sizes[:c]), sizes[c]) for c in range(n)]


def _swiglu(h, w1_ref, w3_ref, w2_ref):
    d_ff = w1_ref.shape[1]
    acc = None
    for start, size in _ff_chunks(d_ff):
        a = jnp.dot(h, w1_ref[:, start:start + size], preferred_element_type=F32)
        b = jnp.dot(h, w3_ref[:, start:start + size], preferred_element_type=F32)
        act = (a * _sigmoid(a) * b).astype(BF16)
        part = jnp.dot(act, w2_ref[start:start + size, :], preferred_element_type=F32)
        acc = part if acc is None else acc + part
    return acc


def _ffn_kernel(x_ref, g_ref, w1_ref, w3_ref, w2_ref, o_ref):
    x = x_ref[...]
    h = _rms(x, g_ref[...]).astype(BF16)
    o_ref[...] = x + _swiglu(h, w1_ref, w3_ref, w2_ref)


def _dense_ffn(x, g, w1, w3, w2):
    t, d = x.shape
    d_ff = w1.shape[1]
    tm = TOKEN_TILE
    row = lambda i: (i, 0)
    fixed = lambda i: (0, 0)
    resident = lambda shape: pl.BlockSpec(shape, fixed, pipeline_mode=pl.Buffered(1))
    return pl.pallas_call(
        _ffn_kernel,
        grid=(t // tm,),
        in_specs=[pl.BlockSpec((tm, d), row), pl.BlockSpec((1, d), fixed),
                  resident((d, d_ff)), resident((d, d_ff)), resident((d_ff, d))],
        out_specs=pl.BlockSpec((tm, d), row),
        out_shape=jax.ShapeDtypeStruct((t, d), F32),
        compiler_params=_cparams(("parallel",)),
        name="dense_ffn",
    )(x, g.reshape(1, d), w1, w3, w2)


def _router_kernel(x_ref, g_ref, r_ref, idx_ref, wt_ref, *, n_experts):
    h = _rms(x_ref[...], g_ref[...])
    r = r_ref[...]
    h_hi, r_hi = h.astype(BF16), r.astype(BF16)
    h_lo = (h - h_hi.astype(F32)).astype(BF16)
    r_lo = (r - r_hi.astype(F32)).astype(BF16)
    logits = (jnp.dot(h_hi, r_hi, preferred_element_type=F32) + jnp.dot(h_hi, r_lo, preferred_element_type=F32)
              + jnp.dot(h_lo, r_hi, preferred_element_type=F32))
    lane = lax.broadcasted_iota(jnp.int32, logits.shape, 1).astype(F32)
    logits = jnp.where(lane < n_experts, logits, -jnp.inf)
    v1 = jnp.max(logits, axis=1, keepdims=True)
    i1 = jnp.min(jnp.where(logits == v1, lane, float(LANES)), axis=1, keepdims=True)
    rest = jnp.where(lane == i1, -jnp.inf, logits)
    v2 = jnp.max(rest, axis=1, keepdims=True)
    i2 = jnp.min(jnp.where(rest == v2, lane, float(LANES)), axis=1, keepdims=True)
    e = jnp.exp(v2 - v1)
    w1 = 1.0 / (1.0 + e)
    w2 = e / (1.0 + e)
    idx_ref[...] = jnp.where(lane == 0.0, i1, jnp.where(lane == 1.0, i2, 0.0))[:, :8].astype(jnp.int32)
    wt_ref[...] = jnp.where(lane == 0.0, w1, jnp.where(lane == 1.0, w2, 0.0))[:, :8]


def _router(x, g, router):
    t, d = x.shape
    n_experts = router.shape[1]
    tm = TOKEN_TILE
    r_pad = jnp.zeros((d, LANES), F32).at[:, :n_experts].set(router.astype(F32))
    row = lambda i: (i, 0)
    fixed = lambda i: (0, 0)
    return pl.pallas_call(
        functools.partial(_router_kernel, n_experts=n_experts),
        grid=(t // tm,),
        in_specs=[pl.BlockSpec((tm, d), row), pl.BlockSpec((1, d), fixed), pl.BlockSpec((d, LANES), fixed)],
        out_specs=[pl.BlockSpec((tm, 8), row), pl.BlockSpec((tm, 8), row)],
        out_shape=[jax.ShapeDtypeStruct((t, 8), jnp.int32), jax.ShapeDtypeStruct((t, 8), F32)],
        compiler_params=_cparams(("parallel",)),
        name="router",
    )(x, g.reshape(1, d), r_pad)


def _row_copy(src, src_row, dst, dst_row, sem):
    return pltpu.make_async_copy(src.at[pl.ds(src_row, 1), :], dst.at[pl.ds(dst_row, 1), :], sem)


MOE_PARTS = 4


def _moe_kernel(te_ref, src_ref, srcn_ref, dst_ref, x_hbm, g_ref, w1_ref, w3_ref, w2_ref, o_hbm,
                xb0, xb1, xb2, xb3, yb0, yb1, yb2, yb3, gsem, ssem, *, tm, n_tiles):
    i = pl.program_id(0)
    last = pl.num_programs(0) - 1
    n_used = te_ref[n_tiles]
    rows = tm // MOE_PARTS
    xbufs, ybufs = (xb0, xb1, xb2, xb3), (yb0, yb1, yb2, yb3)

    def start_gather(idx_ref, part, s):
        for j in range(rows):
            _row_copy(x_hbm, idx_ref[0, part * rows + j], xbufs[s], j, gsem.at[s]).start()

    def wait_gather(s):
        for j in range(rows):
            _row_copy(x_hbm, 0, xbufs[s], j, gsem.at[s]).wait()

    def start_scatter(s):
        for j in range(rows):
            _row_copy(ybufs[s], j, o_hbm, dst_ref[0, s * rows + j], ssem.at[s]).start()

    def wait_scatter(s):
        for j in range(rows):
            _row_copy(ybufs[s], j, o_hbm, 0, ssem.at[s]).wait()

    def compute(s):
        h = _rms(xbufs[s][...], g_ref[...]).astype(BF16)
        ybufs[s][...] = _swiglu(h, w1_ref, w3_ref, w2_ref)

    @pl.when((i > 0) & (i <= n_used))
    def _():
        wait_scatter(2)
        wait_scatter(3)

    @pl.when(i < n_used)
    def _():
        @pl.when(i == 0)
        def _():
            start_gather(src_ref, 0, 0)
            wait_gather(0)

        start_gather(src_ref, 1, 1)
        compute(0)
        wait_gather(1)
        start_gather(src_ref, 2, 2)
        start_scatter(0)
        compute(1)
        wait_gather(2)
        start_gather(src_ref, 3, 3)
        start_scatter(1)
        start_gather(srcn_ref, 0, 0)
        compute(2)
        wait_gather(3)
        wait_gather(0)
        wait_scatter(0)
        wait_scatter(1)
        start_scatter(2)
        compute(3)
        start_scatter(3)

        @pl.when(i == last)
        def _():
            wait_scatter(2)
            wait_scatter(3)

    @pl.when(i >= n_used)
    def _():
        yb0[...] = jnp.zeros(yb0.shape, F32)
        fills = [pltpu.make_async_copy(
            yb0, o_hbm.at[pl.ds(pl.multiple_of(dst_ref[0, part * rows], SUBLANES), rows), :], ssem.at[0])
            for part in range(MOE_PARTS)]
        for fill in fills:
            fill.start()
        for fill in fills:
            fill.wait()


def _moe_plan(idx, n_experts, tm):
    i32 = jnp.int32
    t = idx.shape[0]
    n_assign = TOP_K * t
    pad = n_experts * tm
    n_rows = n_assign + pad
    n_tiles = n_rows // tm
    flat = idx.reshape(-1)
    experts = jnp.arange(n_experts, dtype=i32)
    keys = jnp.sort(flat * n_assign + jnp.arange(n_assign, dtype=i32))
    counts = jnp.sum((flat[:, None] == experts[None, :]).astype(i32), axis=0)
    padded = (counts + tm - 1) // tm * tm
    ends = jnp.cumsum(padded)
    off = jnp.cumsum(counts) - counts
    poff = ends - padded
    rows = jnp.arange(n_rows, dtype=i32)
    e_row = jnp.minimum(jnp.sum((rows[:, None] >= ends[None, :]).astype(i32), axis=1), n_experts - 1)
    onehot = (e_row[:, None] == experts[None, :]).astype(i32)
    rel = rows - jnp.sum(onehot * poff[None, :], axis=1)
    count_row = jnp.sum(onehot * counts[None, :], axis=1)
    valid = rel < count_row
    keys_padded = jnp.concatenate([keys, jnp.zeros((pad,), i32)])
    key_row = jnp.zeros((n_rows,), i32)
    for e in range(n_experts):
        key_row = jnp.where(e_row == e, jnp.roll(keys_padded, poff[e] - off[e]), key_row)
    assign = key_row - e_row * n_assign
    token, slot = assign // TOP_K, assign % TOP_K
    valid_before = jnp.sum(onehot * off[None, :], axis=1) + jnp.minimum(rel, count_row)
    src = jnp.where(valid, token, 0)
    dst = jnp.where(valid, slot * t + token, n_assign + rows - valid_before)
    n_valid = ends[-1] // tm
    tile_e = e_row.reshape(n_tiles, tm)[:, 0]
    tile_e = jnp.where(jnp.arange(n_tiles, dtype=i32) < n_valid, tile_e, tile_e[n_valid - 1])
    tile_info = jnp.concatenate([tile_e, n_valid.reshape(1)]).astype(i32)
    return src.reshape(n_tiles, 1, tm), dst.reshape(n_tiles, 1, tm), tile_info


def _moe_experts(x, g, w1, w3, w2, src, dst, tile_e):
    t, d = x.shape
    d_ff = w1.shape[2]
    n_tiles, _, tm = src.shape
    smem_row = lambda fn: pl.BlockSpec((None, 1, tm), fn, memory_space=pltpu.SMEM)
    expert = lambda shape: pl.BlockSpec((None,) + shape, lambda i, te: (te[i], 0, 0))
    grid_spec = pltpu.PrefetchScalarGridSpec(
        num_scalar_prefetch=1,
        grid=(n_tiles,),
        in_specs=[smem_row(lambda i, te: (i, 0, 0)),
                  smem_row(lambda i, te: (jnp.minimum(i + 1, n_tiles - 1), 0, 0)),
                  smem_row(lambda i, te: (i, 0, 0)),
                  pl.BlockSpec(memory_space=pl.ANY),
                  pl.BlockSpec((1, d), lambda i, te: (0, 0)),
                  expert((d, d_ff)), expert((d, d_ff)), expert((d_ff, d))],
        out_specs=pl.BlockSpec(memory_space=pl.ANY),
        scratch_shapes=[pltpu.VMEM((tm // MOE_PARTS, d), F32)] * (2 * MOE_PARTS)
        + [pltpu.SemaphoreType.DMA((MOE_PARTS,)), pltpu.SemaphoreType.DMA((MOE_PARTS,))],
    )
    return pl.pallas_call(
        functools.partial(_moe_kernel, tm=tm, n_tiles=n_tiles),
        grid_spec=grid_spec,
        out_shape=jax.ShapeDtypeStruct((n_tiles * tm, d), F32),
        compiler_params=_cparams(("arbitrary",)),
        name="moe_experts",
    )(tile_e, src, src, dst, x, g.reshape(1, d), w1, w3, w2)


def _combine_kernel(x_ref, y0_ref, y1_ref, wt_ref, g_ref, o_ref, *, final_norm):
    wt = wt_ref[...]
    out = x_ref[...] + wt[:, 0:1] * y0_ref[...] + wt[:, 1:2] * y1_ref[...]
    if final_norm:
        out = _rms(out, g_ref[...])
    o_ref[...] = out


def _moe_combine(x, y2, wt, g_final, final_norm):
    t, d = x.shape
    tm = TOKEN_TILE
    row = lambda i: (i, 0)
    return pl.pallas_call(
        functools.partial(_combine_kernel, final_norm=final_norm),
        grid=(t // tm,),
        in_specs=[pl.BlockSpec((tm, d), row), pl.BlockSpec((tm, d), row),
                  pl.BlockSpec((tm, d), lambda i: (i + t // tm, 0)),
                  pl.BlockSpec((tm, 8), row), pl.BlockSpec((1, d), lambda i: (0, 0))],
        out_specs=pl.BlockSpec((tm, d), row),
        out_shape=jax.ShapeDtypeStruct((t, d), F32),
        compiler_params=_cparams(("parallel",)),
        name="moe_combine",
    )(x, y2, y2, wt, g_final.reshape(1, d))


def _norm_kernel(x_ref, g_ref, o_ref):
    o_ref[...] = _rms(x_ref[...], g_ref[...])


def _final_norm(x, g):
    t, d = x.shape
    tm = TOKEN_TILE
    return pl.pallas_call(
        _norm_kernel,
        grid=(t // tm,),
        in_specs=[pl.BlockSpec((tm, d), lambda i: (i, 0)), pl.BlockSpec((1, d), lambda i: (0, 0))],
        out_specs=pl.BlockSpec((tm, d), lambda i: (i, 0)),
        out_shape=jax.ShapeDtypeStruct((t, d), F32),
        compiler_params=_cparams(("parallel",)),
        name="final_norm",
    )(x, g.reshape(1, d))


def _trunk(x, p):
    b, s, d = x.shape
    t = b * s
    aw = p['attn_out_norm'].shape[1]
    depth = p['w_in'].shape[0]
    xt = x.reshape(t, d)
    normed = False
    for layer in range(depth):
        q, k, v, uc = _inproj(xt, p['norm_mix'][layer], p['w_in'][layer], aw)
        shape3 = lambda z: z.reshape(b, s, z.shape[1])
        attn = _attention(shape3(q), shape3(k), shape3(v), p['bias_tiles'][s]).reshape(t, aw)
        w1m, w2m, coef = p['ssm'][layer]
        yc = _ssm_scan(uc, w1m, w2m, coef, p['ssm_d'][layer], b)
        xt = _mixout(xt, attn, yc, p['glu_w'][layer], p['glu_b'][layer], p['attn_out_norm'][layer],
                     p['ssm_out_norm'][layer], p['w_out'][layer])
        g = p['norm_ffn'][layer]
        j = layer // 2
        if layer % 2 == 0:
            xt = _dense_ffn(xt, g, p['ffn_w1'][j], p['ffn_w3'][j], p['ffn_w2'][j])
        else:
            idx, wt = _router(xt, g, p['moe_router'][j])
            n_experts = p['moe_router'].shape[2]
            src, dst, tile_e = _moe_plan(idx[:, :TOP_K], n_experts, MOE_TILE)
            y2 = _moe_experts(xt, g, p['moe_w1'][j], p['moe_w3'][j], p['moe_w2'][j], src, dst, tile_e)
            last = layer == depth - 1
            xt = _moe_combine(xt, y2, wt, p['final_norm'], last)
            normed = last
    if not normed:
        xt = _final_norm(xt, p['final_norm'])
    return xt.reshape(b, s, d)


def kernel(x_prompt, x_sample, rel_bias, norm_mix, w_in, ssm_lambda_re, ssm_lambda_im, ssm_log_step, ssm_b_re, ssm_b_im, ssm_c_re, ssm_c_im, ssm_d, glu_w, glu_b, attn_out_norm, ssm_out_norm, w_out, norm_ffn, ffn_w1, ffn_w3, ffn_w2, moe_router, moe_w1, moe_w3, moe_w2, final_norm):
    depth = w_in.shape[0]
    bias_tiles = {}
    for x in (x_prompt, x_sample):
        s = x.shape[1]
        if s not in bias_tiles:
            bias_tiles[s] = [_bias_tiles(rel_bias, dil, s // dil) for dil in DILATIONS]
    p = {
        'bias_tiles': bias_tiles,
        'norm_mix': norm_mix, 'w_in': w_in.astype(BF16),
        'ssm': [_ssm_matrices(ssm_lambda_re[l], ssm_lambda_im[l], ssm_log_step[l], ssm_b_re[l], ssm_b_im[l],
                              ssm_c_re[l], ssm_c_im[l]) for l in range(depth)],
        'ssm_d': ssm_d, 'glu_w': glu_w.astype(BF16), 'glu_b': glu_b,
        'attn_out_norm': attn_out_norm, 'ssm_out_norm': ssm_out_norm, 'w_out': w_out.astype(BF16),
        'norm_ffn': norm_ffn,
        'ffn_w1': ffn_w1.astype(BF16), 'ffn_w3': ffn_w3.astype(BF16), 'ffn_w2': ffn_w2.astype(BF16),
        'moe_router': moe_router,
        'moe_w1': moe_w1.astype(BF16), 'moe_w3': moe_w3.astype(BF16), 'moe_w2': moe_w2.astype(BF16),
        'final_norm': final_norm,
    }
    return (_trunk(x_prompt, p), _trunk(x_sample, p))
```
